```python
import jax, jax.numpy as jnp
from jax import lax
import numpy as np

D_MODEL = 1024
BATCH = 2
SEQ = 8192
DEPTH = 1

D_CONV = D_MODEL
CONV_A_WIDTH = 3
GDN_HEADS = D_MODEL // 128
HEAD_K = 128
HEAD_V = 128
KEY_DIM = GDN_HEADS * HEAD_K
VAL_DIM = GDN_HEADS * HEAD_V
CONV_QKV_WIDTH = 5
N_DIR = 2
CHUNK = 64
NORM_EPS = 1e-6
L2_EPS = 1e-6
SPLITS = (D_CONV, D_CONV, D_CONV, D_CONV,
          KEY_DIM, KEY_DIM, VAL_DIM, VAL_DIM,
          N_DIR * GDN_HEADS, N_DIR * GDN_HEADS,
          D_MODEL, D_MODEL)
N_IN = sum(SPLITS)

kernel_name = "hybrid_conv_gdn_bidir_adaln"


def rmsnorm(x, w):
    xf = x.astype(jnp.float32)
    y = xf * lax.rsqrt(jnp.mean(xf * xf, axis=-1, keepdims=True) + NORM_EPS)
    return (y * w.astype(jnp.float32)).astype(x.dtype)


def l2norm(x):
    return x * lax.rsqrt(jnp.sum(x * x, axis=-1, keepdims=True) + L2_EPS)


def dwconv_centred(x, w):
    k = w.shape[0]
    return lax.conv_general_dilated(
        x, w[:, None, :].astype(x.dtype), window_strides=(1,),
        padding=[(k // 2, k // 2)], dimension_numbers=('NWC', 'WIO', 'NWC'),
        feature_group_count=x.shape[-1])


def gated_delta_chunked(q, k, v, g, beta):
    bsz, nh, slen, dk = q.shape
    dv = v.shape[-1]
    nc = slen // CHUNK
    q = q * (dk ** -0.5)
    q = q.reshape(bsz, nh, nc, CHUNK, dk)
    k = k.reshape(bsz, nh, nc, CHUNK, dk)
    v = v.reshape(bsz, nh, nc, CHUNK, dv)
    beta = beta.reshape(bsz, nh, nc, CHUNK)
    g = jnp.cumsum(g.reshape(bsz, nh, nc, CHUNK), axis=-1)
    idx = jnp.arange(CHUNK)
    incl = idx[:, None] >= idx[None, :]
    strict = idx[:, None] > idx[None, :]
    diff = g[..., :, None] - g[..., None, :]
    decay = jnp.where(incl, jnp.exp(jnp.where(incl, diff, 0.0)), 0.0)
    kk = jnp.einsum('bhnid,bhnjd->bhnij', k, k)
    lower = jnp.where(strict, beta[..., :, None] * kk * decay, 0.0)
    eye = jnp.eye(CHUNK, dtype=q.dtype)
    tmat = lax.linalg.triangular_solve(eye + lower, jnp.broadcast_to(eye, lower.shape),
                                       left_side=True, lower=True, unit_diagonal=True)
    eg = jnp.exp(g)
    u = jnp.einsum('bhnij,bhnjd->bhnid', tmat, v * beta[..., None])
    w = jnp.einsum('bhnij,bhnjd->bhnid', tmat, k * (beta * eg)[..., None])
    a_qk = jnp.einsum('bhnid,bhnjd->bhnij', q, k) * decay
    q_dec = q * eg[..., None]
    g_last = g[..., -1]
    k_dec = k * jnp.exp(g_last[..., None] - g)[..., None]
    gl = jnp.exp(g_last)
    xs = tuple(jnp.moveaxis(t, 2, 0) for t in (q_dec, k_dec, u, w, a_qk, gl))

    def step(state, inp):
        qd, kd, uu, ww, aqk, gll = inp
        v_new = uu - jnp.einsum('bhcd,bhde->bhce', ww, state)
        o = jnp.einsum('bhcd,bhde->bhce', qd, state) + jnp.einsum('bhij,bhje->bhie', aqk, v_new)
        state = state * gll[..., None, None] + jnp.einsum('bhcd,bhce->bhde', kd, v_new)
        return state, o

    state0 = jnp.zeros((bsz, nh, dk, dv), jnp.float32)
    _, o = lax.scan(step, state0, xs)
    return jnp.moveaxis(o, 0, 2).reshape(bsz, nh, slen, dv)


def setup_inputs(seed: int = 0) -> dict:
    key = jax.random.key(seed)
    ks = jax.random.split(key, 16)
    f32 = jnp.float32
    x = jax.random.normal(ks[0], (BATCH, SEQ, D_MODEL), f32)
    c = jax.random.normal(ks[1], (BATCH, D_MODEL), f32)
    w_ada = jax.random.normal(ks[2], (DEPTH, D_MODEL, 3 * D_MODEL), f32) * D_MODEL ** -0.5
    b_ada = jax.random.normal(ks[3], (DEPTH, 3 * D_MODEL), f32) * 0.02
    norm_w = 1.0 + 0.01 * jax.random.normal(ks[4], (DEPTH, D_MODEL), f32)
    w_in = jax.random.normal(ks[5], (DEPTH, D_MODEL, N_IN), f32) * D_MODEL ** -0.5
    conv_a_w = jax.random.normal(ks[6], (DEPTH, CONV_A_WIDTH, D_CONV), f32) * CONV_A_WIDTH ** -0.5
    conv_qkv_w = jax.random.normal(ks[7], (DEPTH, CONV_QKV_WIDTH, 2 * KEY_DIM + VAL_DIM), f32) * CONV_QKV_WIDTH ** -0.5
    a_log = jnp.log(jax.random.uniform(ks[8], (DEPTH, N_DIR, GDN_HEADS), f32, 1.0, 16.0))
    dt = jnp.exp(jax.random.uniform(ks[9], (DEPTH, N_DIR, GDN_HEADS), f32,
                                    float(np.log(1e-3)), float(np.log(1e-1))))
    dt_bias = dt + jnp.log(-jnp.expm1(-dt))
    gdn_norm_w = 1.0 + 0.01 * jax.random.normal(ks[10], (DEPTH, HEAD_V), f32)
    w_pa = jax.random.normal(ks[11], (DEPTH, D_CONV, D_MODEL), f32) * D_CONV ** -0.5
    w_pb = jax.random.normal(ks[12], (DEPTH, VAL_DIM, D_MODEL), f32) * VAL_DIM ** -0.5
    w_o = jax.random.normal(ks[13], (DEPTH, D_MODEL, D_MODEL), f32) * D_MODEL ** -0.5
    final_norm_w = 1.0 + 0.01 * jax.random.normal(ks[14], (D_MODEL,), f32)
    return {"x": x, "c": c, "w_ada": w_ada, "b_ada": b_ada, "norm_w": norm_w, "w_in": w_in,
            "conv_a_w": conv_a_w, "conv_qkv_w": conv_qkv_w, "a_log": a_log, "dt_bias": dt_bias,
            "gdn_norm_w": gdn_norm_w, "w_pa": w_pa, "w_pb": w_pb, "w_o": w_o,
            "final_norm_w": final_norm_w}


def reference(x, c, w_ada, b_ada, norm_w, w_in, conv_a_w, conv_qkv_w, a_log, dt_bias,
              gdn_norm_w, w_pa, w_pb, w_o, final_norm_w):
    bsz, slen, _ = x.shape
    dt_ = x.dtype
    split_pts = [int(p) for p in np.cumsum(SPLITS)[:-1]]
    for l in range(DEPTH):
        mod = jax.nn.silu(c) @ w_ada[l] + b_ada[l]
        shift, scale, gate = jnp.split(mod, 3, axis=-1)
        h = rmsnorm(x, norm_w[l]) * (1.0 + scale[:, None, :]) + shift[:, None, :]

        proj = h @ w_in[l]
        (a_bg, a_cg, a_x, a_z, q, k, v, z_b, a_raw, b_raw,
         gate_a_raw, gate_b_raw) = jnp.split(proj, split_pts, axis=-1)

        ya = a_cg * dwconv_centred(a_bg * a_x, conv_a_w[l])
        ya = ya * jax.nn.silu(a_z)
        ya = ya @ w_pa[l]

        qkv = jax.nn.silu(dwconv_centred(jnp.concatenate([q, k, v], axis=-1), conv_qkv_w[l]))
        q, k, v = jnp.split(qkv, [KEY_DIM, 2 * KEY_DIM], axis=-1)
        q = l2norm(q.astype(jnp.float32).reshape(bsz, slen, GDN_HEADS, HEAD_K))
        k = l2norm(k.astype(jnp.float32).reshape(bsz, slen, GDN_HEADS, HEAD_K))
        v = v.astype(jnp.float32).reshape(bsz, slen, GDN_HEADS, HEAD_V)
        a_raw = a_raw.astype(jnp.float32).reshape(bsz, slen, N_DIR, GDN_HEADS)
        b_raw = b_raw.astype(jnp.float32).reshape(bsz, slen, N_DIR, GDN_HEADS)
        g = -jnp.exp(a_log[l].astype(jnp.float32)) * jax.nn.softplus(a_raw + dt_bias[l].astype(jnp.float32))
        beta = jax.nn.sigmoid(b_raw)
        q2 = jnp.concatenate([q, jnp.flip(q, 1)], axis=2)
        k2 = jnp.concatenate([k, jnp.flip(k, 1)], axis=2)
        v2 = jnp.concatenate([v, jnp.flip(v, 1)], axis=2)
        g2 = jnp.concatenate([g[:, :, 0], jnp.flip(g[:, :, 1], 1)], axis=2)
        b2 = jnp.concatenate([beta[:, :, 0], jnp.flip(beta[:, :, 1], 1)], axis=2)
        o = gated_delta_chunked(jnp.swapaxes(q2, 1, 2), jnp.swapaxes(k2, 1, 2),
                                jnp.swapaxes(v2, 1, 2), jnp.swapaxes(g2, 1, 2),
                                jnp.swapaxes(b2, 1, 2))
        o = o[:, :GDN_HEADS] + jnp.flip(o[:, GDN_HEADS:], axis=2)
        o = jnp.swapaxes(o, 1, 2)
        o = o * lax.rsqrt(jnp.mean(o * o, axis=-1, keepdims=True) + NORM_EPS) * gdn_norm_w[l].astype(jnp.float32)
        z = z_b.astype(jnp.float32).reshape(bsz, slen, GDN_HEADS, HEAD_V)
        yb = (o * jax.nn.silu(z)).reshape(bsz, slen, VAL_DIM).astype(dt_)
        yb = yb @ w_pb[l]

        merged = jax.nn.sigmoid(gate_a_raw) * ya + jax.nn.sigmoid(gate_b_raw) * yb
        x = x + gate[:, None, :] * (merged @ w_o[l])
    return rmsnorm(x, final_norm_w)
```

```python
import functools

import jax
import jax.numpy as jnp
from jax import lax
from jax.experimental import pallas as pl
from jax.experimental.pallas import tpu as pltpu

F32 = jnp.float32
BF16 = jnp.bfloat16

HEAD_DIM = 128
CHUNK = 64
N_DIR = 2
CONV_A_WIDTH = 3
CONV_QKV_WIDTH = 5
NORM_EPS = 1e-6
L2_EPS = 1e-6
BF16_SUBLANES = 16
VMEM_LIMIT_BYTES = 56 * 1024 * 1024

INPROJ_ROWS = 512
PREP_ROWS = 512
OUT_ROWS = 256


def _dot(a, b):
    return jnp.dot(a, b, preferred_element_type=F32)


def _dot_nt(a, b):
    return lax.dot_general(a, b, (((1,), (1,)), ((), ())), preferred_element_type=F32)


def _silu(x):
    return x * jax.nn.sigmoid(x)


def _ada_kernel(c_ref, w_ref, b_ref, o_ref):
    sc = _silu(c_ref[...])
    o_ref[...] = jnp.dot(sc, w_ref[...], preferred_element_type=F32,
                         precision=lax.Precision.HIGHEST) + b_ref[...]


def _ada(c_pad, w, b):
    m, d = c_pad.shape
    n = w.shape[1]
    bn = d
    return pl.pallas_call(
        _ada_kernel,
        grid=(n // bn,),
        in_specs=[pl.BlockSpec((m, d), lambda j: (0, 0)),
                  pl.BlockSpec((d, bn), lambda j: (0, j)),
                  pl.BlockSpec((1, bn), lambda j: (0, j))],
        out_specs=pl.BlockSpec((m, bn), lambda j: (0, j)),
        out_shape=jax.ShapeDtypeStruct((m, n), F32),
        name="ada",
    )(c_pad, w, b)


def _seg_cumsum(x, axis, reverse):
    n = x.shape[axis]
    pos = lax.broadcasted_iota(jnp.int32, x.shape, axis) % CHUNK
    s = 1
    while s < CHUNK:
        if reverse:
            x = x + jnp.where(pos < CHUNK - s, pltpu.roll(x, n - s, axis), 0.0)
        else:
            x = x + jnp.where(pos >= s, pltpu.roll(x, s, axis), 0.0)
        s *= 2
    return x


def _decay_and_beta(ab, alog, dtb, axis, nh):
    pre = ab + dtb
    softplus = jnp.maximum(pre, 0.0) + jnp.log(1.0 + jnp.exp(-jnp.abs(pre)))
    g = -jnp.exp(alog) * softplus
    beta = jax.nn.sigmoid(ab)
    col = lax.broadcasted_iota(jnp.int32, ab.shape, 1 - axis)
    gf = _seg_cumsum(g, axis, False)
    gb = _seg_cumsum(g, axis, True)
    return jnp.where(col < nh, gf, jnp.where(col < N_DIR * nh, gb, beta))


def _inproj_kernel(x_ref, sc_ref, sh_ref, nw_ref, wbg, wcg, wax, waz, wqkv, wz, wga, wgb, wab, wabt,
                   alog, dtb, alogt, dtbt,
                   p_ref, gz_ref, qkv_ref, zs_ref, sga_ref, sgb_ref, gcs_ref, gcst_ref, *, nh, cw):
    x = x_ref[...]
    h = x * lax.rsqrt(jnp.mean(x * x, axis=-1, keepdims=True) + NORM_EPS) * nw_ref[...]
    h = h * (1.0 + sc_ref[0]) + sh_ref[0]
    hb = h.astype(BF16)

    d_conv = p_ref.shape[1]
    for c0 in range(0, d_conv, cw):
        sl = slice(c0, c0 + cw)
        p_ref[:, sl] = (_dot(hb, wbg[:, sl]) * _dot(hb, wax[:, sl])).astype(BF16)
        gz_ref[:, sl] = (_dot(hb, wcg[:, sl]) * _silu(_dot(hb, waz[:, sl]))).astype(BF16)
    for c0 in range(0, qkv_ref.shape[1], cw):
        sl = slice(c0, c0 + cw)
        qkv_ref[:, sl] = _dot(hb, wqkv[:, sl]).astype(BF16)
    for c0 in range(0, zs_ref.shape[1], cw):
        sl = slice(c0, c0 + cw)
        zs_ref[:, sl] = _silu(_dot(hb, wz[:, sl])).astype(BF16)
    for c0 in range(0, sga_ref.shape[1], cw):
        sl = slice(c0, c0 + cw)
        sga_ref[:, sl] = jax.nn.sigmoid(_dot(hb, wga[:, sl])).astype(BF16)
        sgb_ref[:, sl] = jax.nn.sigmoid(_dot(hb, wgb[:, sl])).astype(BF16)

    gcs_ref[...] = _decay_and_beta(_dot(hb, wab[...]), alog[...], dtb[...], 0, nh)
    gcst_ref[...] = _decay_and_beta(_dot_nt(wabt[...], hb), alogt[...], dtbt[...], 1, nh)


def _inproj(x2, scale, shift, norm_w, ws, alog, dtb, *, seq, nh):
    rows, d = x2.shape
    tm = INPROJ_ROWS
    assert seq % tm == 0 and tm % CHUNK == 0
    tiles_per_seq = seq // tm
    wbg, wcg, wax, waz, wqkv, wz, wga, wgb, wab = ws
    nab = wab.shape[1]
    wabt = wab.T
    whole = pl.BlockSpec(memory_space=pltpu.VMEM)
    row_blk = lambda w: pl.BlockSpec((tm, w), lambda i: (i, 0))
    mod_blk = pl.BlockSpec((1, 1, d), lambda i: (i // tiles_per_seq, 0, 0))
    out_shapes = [jax.ShapeDtypeStruct((rows, w.shape[1]), BF16) for w in (wbg, wcg, wqkv, wz, wga, wgb)]
    out_shapes += [jax.ShapeDtypeStruct((rows, nab), F32), jax.ShapeDtypeStruct((nab, rows), F32)]
    out_specs = [row_blk(s.shape[1]) for s in out_shapes[:-1]]
    out_specs += [pl.BlockSpec((nab, tm), lambda i: (0, i))]
    return pl.pallas_call(
        functools.partial(_inproj_kernel, nh=nh, cw=512),
        grid=(rows // tm,),
        in_specs=[row_blk(d), mod_blk, mod_blk, whole] + [whole] * 10 + [whole] * 4,
        out_specs=out_specs,
        out_shape=out_shapes,
        compiler_params=pltpu.CompilerParams(dimension_semantics=("arbitrary",),
                                             vmem_limit_bytes=VMEM_LIMIT_BYTES),
        name="inproj",
    )(x2, scale, shift, norm_w, wbg, wcg, wax, waz, wqkv, wz, wga, wgb, wab, wabt,
      alog, dtb, alog.T, dtb.T)


def _conv_rows(xm, prev, nxt, w_ref, width):
    r = xm.shape[0]
    hal = prev.shape[0]
    row = lax.broadcasted_iota(jnp.int32, xm.shape, 0)
    half = width // 2
    acc = xm * w_ref[half:half + 1, :]
    for t in range(1, half + 1):
        dn = pltpu.roll(xm, t, 0)
        for j in range(t):
            src = hal + j - t
            dn = jnp.where(row == j, prev[src:src + 1, :], dn)
        acc = acc + dn * w_ref[half - t:half - t + 1, :]
        up = pltpu.roll(xm, r - t, 0)
        for j in range(t):
            up = jnp.where(row == r - t + j, nxt[j:j + 1, :], up)
        acc = acc + up * w_ref[half + t:half + t + 1, :]
    return acc


def _halo_specs(tile_rows, width_blk, col_of, row_tile_of, total_rows):
    per = tile_rows // BF16_SUBLANES
    last = total_rows // BF16_SUBLANES - 1

    def prev_map(*g):
        return (jnp.maximum(row_tile_of(*g) * per - 1, 0), col_of(*g))

    def next_map(*g):
        return (jnp.minimum((row_tile_of(*g) + 1) * per, last), col_of(*g))

    return (pl.BlockSpec((BF16_SUBLANES, width_blk), prev_map),
            pl.BlockSpec((BF16_SUBLANES, width_blk), next_map))


BASE_LOG2 = 3


def _unit_triangular_inverse(pmat, ii, jj):
    same_block = lambda s: (ii >> s) == (jj >> s)
    p = jnp.where(same_block(BASE_LOG2), pmat, 0.0)
    x = jnp.where(ii == jj, 1.0, 0.0) + p
    pb = p.astype(BF16)
    p2 = _dot(pb, pb)
    y = _dot(p2.astype(BF16), jnp.concatenate([x, p2], axis=1).astype(BF16))
    x = x + y[:, :CHUNK]
    x = x + _dot(y[:, CHUNK:].astype(BF16), x.astype(BF16))
    s = BASE_LOG2
    while (1 << s) < CHUNK:
        off = jnp.where(same_block(s + 1) & jnp.logical_not(same_block(s)), pmat, 0.0)
        xb = x.astype(BF16)
        x = x + _dot(xb, _dot(off.astype(BF16), xb).astype(BF16))
        s += 1
    return x

def _prep_kernel(qm, qp, qn, km, kp, kn, vm, vp, vn, cwq, cwk, cwv, gcs_ref, gcst_ref,
                 lhs1_ref, lhs2_ref, u_ref, gl_ref, *, nrb, nh):
    r = pl.program_id(1)
    h = pl.program_id(2)
    first = r == 0
    last = r == nrb - 1

    def conv_silu(m, p, n, cw):
        pv = jnp.where(first, 0.0, p[...].astype(F32))
        nx = jnp.where(last, 0.0, n[...].astype(F32))
        return _silu(_conv_rows(m[...].astype(F32), pv, nx, cw, CONV_QKV_WIDTH))

    q = conv_silu(qm, qp, qn, cwq)
    k = conv_silu(km, kp, kn, cwk)
    v = conv_silu(vm, vp, vn, cwv)
    q = q * lax.rsqrt(jnp.sum(q * q, axis=-1, keepdims=True) + L2_EPS) * (HEAD_DIM ** -0.5)
    k = k * lax.rsqrt(jnp.sum(k * k, axis=-1, keepdims=True) + L2_EPS)

    gcs = gcs_ref[...]
    lane = lax.broadcasted_iota(jnp.int32, gcs.shape, 1)

    def pick(col):
        return jnp.sum(jnp.where(lane == col, gcs, 0.0), axis=1, keepdims=True)

    g_col = (pick(h), pick(nh + h))
    b_col = (pick(2 * nh + h), pick(3 * nh + h))
    g_row = (gcst_ref[pl.ds(h, 1), :], gcst_ref[pl.ds(nh + h, 1), :])

    ii = lax.broadcasted_iota(jnp.int32, (CHUNK, CHUNK), 0)
    jj = lax.broadcasted_iota(jnp.int32, (CHUNK, CHUNK), 1)

    for c in range(q.shape[0] // CHUNK):
        rs = slice(c * CHUNK, (c + 1) * CHUNK)
        kc, qc, vc = k[rs], q[rs], v[rs]
        kq = _dot_nt(jnp.concatenate([kc, qc], axis=0).astype(BF16), kc.astype(BF16))
        kk, qk = kq[:CHUNK], kq[CHUNK:]
        for d in range(N_DIR):
            gc = g_col[d][rs]
            gr = g_row[d][:, rs]
            bc = b_col[d][rs]
            incl = (ii >= jj) if d == 0 else (ii <= jj)
            strict = (ii > jj) if d == 0 else (ii < jj)
            dm = jnp.where(incl, jnp.exp(jnp.where(incl, gc - gr, 0.0)), 0.0)
            pmat = jnp.where(strict, -(bc * kk * dm), 0.0)
            aqk = qk * dm
            tmat = _unit_triangular_inverse(pmat, ii, jj)

            eg = jnp.exp(gc)
            rhs = jnp.concatenate([vc * bc, kc * (bc * eg)], axis=1).astype(BF16)
            uw = _dot(tmat.astype(BF16), rhs)
            u, w = uw[:, :HEAD_DIM], uw[:, HEAD_DIM:]
            g_tot = gc[CHUNK - 1:CHUNK] if d == 0 else gc[0:1]
            kd = kc * jnp.exp(g_tot - gc)
            qd = qc * eg
            lhs1_ref[d, c] = jnp.concatenate([w, qd], axis=0).astype(BF16)
            lhs2_ref[d, c] = jnp.concatenate([aqk, kd.T], axis=0).astype(BF16)
            u_ref[d, c] = u.astype(BF16)
            gl_ref[d, c] = jnp.broadcast_to(jnp.exp(g_tot), (1, HEAD_DIM))


def _prep(qkv, conv_w, gcs, gcst, *, bsz, seq, nh):
    rows = qkv.shape[0]
    rr = PREP_ROWS
    nrb = seq // rr
    ncb = rr // CHUNK
    nc = seq // CHUNK
    ngc = gcs.shape[1]
    row_of = lambda b, r, h: b * nrb + r
    in_specs = []
    for part in range(3):
        col_of = functools.partial(lambda b, r, h, part: part * nh + h, part=part)
        in_specs.append(pl.BlockSpec((rr, HEAD_DIM), functools.partial(
            lambda b, r, h, part: (b * nrb + r, part * nh + h), part=part)))
        in_specs.extend(_halo_specs(rr, HEAD_DIM, col_of, row_of, rows))
    for part in range(3):
        in_specs.append(pl.BlockSpec((CONV_QKV_WIDTH, HEAD_DIM), functools.partial(
            lambda b, r, h, part: (0, part * nh + h), part=part)))
    in_specs.append(pl.BlockSpec((rr, ngc), lambda b, r, h: (b * nrb + r, 0)))
    in_specs.append(pl.BlockSpec((ngc, rr), lambda b, r, h: (0, b * nrb + r)))

    def out(shape_tail, dtype):
        blk = (None, N_DIR, None, ncb) + shape_tail
        zeros = (0,) * len(shape_tail)
        return (jax.ShapeDtypeStruct((bsz, N_DIR, nh, nc) + shape_tail, dtype),
                pl.BlockSpec(blk, lambda b, r, h: (b, 0, h, r) + zeros))

    outs = [out((2 * CHUNK, HEAD_DIM), BF16), out((CHUNK + HEAD_DIM, CHUNK), BF16),
            out((CHUNK, HEAD_DIM), BF16), out((1, HEAD_DIM), F32)]
    return pl.pallas_call(
        functools.partial(_prep_kernel, nrb=nrb, nh=nh),
        grid=(bsz, nrb, nh),
        in_specs=in_specs,
        out_specs=[o[1] for o in outs],
        out_shape=[o[0] for o in outs],
        compiler_params=pltpu.CompilerParams(dimension_semantics=("arbitrary",) * 3,
                                             vmem_limit_bytes=VMEM_LIMIT_BYTES),
        name="prep",
    )(qkv, qkv, qkv, qkv, qkv, qkv, qkv, qkv, qkv, conv_w, conv_w, conv_w, gcs, gcst)


def _scan_kernel(l1f, l2f, uf, glf, l1b, l2b, ub, glb, of_ref, ob_ref, s_ref, *, ncb, nh):
    @pl.when(pl.program_id(1) == 0)
    def _():
        s_ref[...] = jnp.zeros_like(s_ref)

    dirs = ((l1f, l2f, uf, glf, of_ref), (l1b, l2b, ub, glb, ob_ref))

    def body(ci, carry):
        for d in range(N_DIR):
            l1, l2, uu, gl, o_ref = dirs[d]
            cc = ci if d == 0 else ncb - 1 - ci
            r0 = pl.multiple_of(cc * CHUNK, CHUNK)
            for h in range(nh):
                s = s_ref[d, h]
                r1 = _dot(l1[h, cc], s.astype(BF16))
                v_new = uu[h, cc].astype(F32) - r1[:CHUNK]
                r2 = _dot(l2[h, cc], v_new.astype(BF16))
                s_ref[d, h] = s * gl[h, cc] + r2[CHUNK:]
                o_ref[pl.ds(r0, CHUNK), h * HEAD_DIM:(h + 1) * HEAD_DIM] = r1[CHUNK:] + r2[:CHUNK]
        return carry

    lax.fori_loop(0, ncb, body, 0)


def _scan(lhs1, lhs2, u, gl, *, seq):
    bsz, _, nh, nc = lhs1.shape[:4]
    rr = PREP_ROWS
    ncb = rr // CHUNK
    nt = nc // ncb

    def specs(d):
        def mk(arr):
            tail = arr.shape[4:]
            zeros = (0,) * len(tail)
            if d == 0:
                imap = lambda b, t: (b, 0, 0, t) + zeros
            else:
                imap = lambda b, t: (b, 1, 0, nt - 1 - t) + zeros
            return pl.BlockSpec((None, None, nh, ncb) + tail, imap)
        return [mk(a) for a in (lhs1, lhs2, u, gl)]

    o_shape = jax.ShapeDtypeStruct((bsz * seq, nh * HEAD_DIM), F32)
    return pl.pallas_call(
        functools.partial(_scan_kernel, ncb=ncb, nh=nh),
        grid=(bsz, nt),
        in_specs=specs(0) + specs(1),
        out_specs=[pl.BlockSpec((rr, nh * HEAD_DIM), lambda b, t: (b * nt + t, 0)),
                   pl.BlockSpec((rr, nh * HEAD_DIM), lambda b, t: (b * nt + nt - 1 - t, 0))],
        out_shape=[o_shape, o_shape],
        scratch_shapes=[pltpu.VMEM((N_DIR, nh, HEAD_DIM, HEAD_DIM), F32)],
        compiler_params=pltpu.CompilerParams(dimension_semantics=("arbitrary", "arbitrary"),
                                             vmem_limit_bytes=VMEM_LIMIT_BYTES),
        name="scan",
    )(lhs1, lhs2, u, gl, lhs1, lhs2, u, gl)


def _out_kernel(x_ref, of_ref, ob_ref, zs_ref, pm, pp, pn, gz_ref, sga_ref, sgb_ref, gate_ref,
                cwa_ref, gnw_ref, fnw_ref, wpa, wpb, wo, y_ref, *, tiles_per_seq, nh, final):
    r = pl.program_id(0) % tiles_per_seq
    o = of_ref[...] + ob_ref[...]
    heads = []
    for h in range(nh):
        oh = o[:, h * HEAD_DIM:(h + 1) * HEAD_DIM]
        heads.append(oh * lax.rsqrt(jnp.mean(oh * oh, axis=-1, keepdims=True) + NORM_EPS) * gnw_ref[...])
    yb_in = jnp.concatenate(heads, axis=1) * zs_ref[...].astype(F32)
    yb = _dot(yb_in.astype(BF16), wpb[...])

    pv = jnp.where(r == 0, 0.0, pp[...].astype(F32))
    nx = jnp.where(r == tiles_per_seq - 1, 0.0, pn[...].astype(F32))
    conv = _conv_rows(pm[...].astype(F32), pv, nx, cwa_ref, CONV_A_WIDTH)
    ya = _dot((gz_ref[...].astype(F32) * conv).astype(BF16), wpa[...])

    merged = sga_ref[...].astype(F32) * ya + sgb_ref[...].astype(F32) * yb
    xn = x_ref[...] + gate_ref[0] * _dot(merged.astype(BF16), wo[...])
    if final:
        xn = xn * lax.rsqrt(jnp.mean(xn * xn, axis=-1, keepdims=True) + NORM_EPS) * fnw_ref[...]
    y_ref[...] = xn


def _out(x2, o_f, o_b, zs, p, gz, sga, sgb, gate, conv_a_w, gdn_norm_w, final_norm_w, wpa, wpb, wo,
         *, seq, nh, final):
    rows, d = x2.shape
    tm = OUT_ROWS
    tiles_per_seq = seq // tm
    row_blk = lambda w: pl.BlockSpec((tm, w), lambda i: (i, 0))
    const = lambda a: pl.BlockSpec(a.shape, lambda i: (0,) * a.ndim)
    halo_prev, halo_next = _halo_specs(tm, p.shape[1], lambda i: 0, lambda i: i, rows)
    return pl.pallas_call(
        functools.partial(_out_kernel, tiles_per_seq=tiles_per_seq, nh=nh, final=final),
        grid=(rows // tm,),
        in_specs=[row_blk(d), row_blk(o_f.shape[1]), row_blk(o_b.shape[1]), row_blk(zs.shape[1]),
                  row_blk(p.shape[1]), halo_prev, halo_next, row_blk(gz.shape[1]),
                  row_blk(sga.shape[1]), row_blk(sgb.shape[1]),
                  pl.BlockSpec((1, 1, d), lambda i: (i // tiles_per_seq, 0, 0)),
                  const(conv_a_w), const(gdn_norm_w), const(final_norm_w),
                  const(wpa), const(wpb), const(wo)],
        out_specs=row_blk(d),
        out_shape=jax.ShapeDtypeStruct((rows, d), F32),
        compiler_params=pltpu.CompilerParams(dimension_semantics=("arbitrary",),
                                             vmem_limit_bytes=VMEM_LIMIT_BYTES),
        name="out",
    )(x2, o_f, o_b, zs, p, p, p, gz, sga, sgb, gate, conv_a_w, gdn_norm_w, final_norm_w, wpa, wpb, wo)


def kernel(x, c, w_ada, b_ada, norm_w, w_in, conv_a_w, conv_qkv_w, a_log, dt_bias, gdn_norm_w,
           w_pa, w_pb, w_o, final_norm_w):
    bsz, seq, d = x.shape
    depth = w_ada.shape[0]
    nh = a_log.shape[2]
    d_conv = conv_a_w.shape[2]
    key_dim = nh * HEAD_DIM
    assert conv_qkv_w.shape[2] == 3 * key_dim and gdn_norm_w.shape[1] == HEAD_DIM
    assert seq % PREP_ROWS == 0 and seq % OUT_ROWS == 0
    n_gate = N_DIR * nh
    splits = (d_conv,) * 4 + (key_dim,) * 4 + (n_gate, n_gate, d, d)
    assert sum(splits) == w_in.shape[2]
    offs = [0]
    for s in splits:
        offs.append(offs[-1] + s)

    x2 = x.reshape(bsz * seq, d)
    c_pad = jnp.zeros((8, d), F32).at[:bsz].set(c)
    for l in range(depth):
        mod = _ada(c_pad, w_ada[l], b_ada[l][None, :])[:bsz]
        shift, scale, gate = (mod[:, i * d:(i + 1) * d][:, None, :] for i in range(3))

        wb = w_in[l].astype(BF16)
        col = lambda a, b: wb[:, offs[a]:offs[b]]
        ws = (col(0, 1), col(1, 2), col(2, 3), col(3, 4), col(4, 7), col(7, 8), col(10, 11), col(11, 12),
              col(8, 10))
        zeros = jnp.zeros((1, n_gate), F32)
        alog = jnp.concatenate([a_log[l].reshape(1, n_gate), zeros], axis=1)
        dtb = jnp.concatenate([dt_bias[l].reshape(1, n_gate), zeros], axis=1)
        p, gz, qkv, zs, sga, sgb, gcs, gcst = _inproj(
            x2, scale, shift, norm_w[l][None, :], ws, alog, dtb, seq=seq, nh=nh)

        lhs1, lhs2, u, gl = _prep(qkv, conv_qkv_w[l], gcs, gcst, bsz=bsz, seq=seq, nh=nh)
        o_f, o_b = _scan(lhs1, lhs2, u, gl, seq=seq)

        x2 = _out(x2, o_f, o_b, zs, p, gz, sga, sgb, gate, conv_a_w[l], gdn_norm_w[l][None, :],
                  final_norm_w[None, :], w_pa[l].astype(BF16), w_pb[l].astype(BF16), w_o[l].astype(BF16),
                  seq=seq, nh=nh, final=(l == depth - 1))
    return x2.reshape(bsz, seq, d)
```

```python
import functools

import jax
import jax.numpy as jnp
from jax import lax
from jax.experimental import pallas as pl
from jax.experimental.pallas import tpu as pltpu

F32 = jnp.float32
BF16 = jnp.bfloat16

HEAD_DIM = 128
CHUNK = 64
N_DIR = 2
CONV_A_WIDTH = 3
CONV_QKV_WIDTH = 5
NORM_EPS = 1e-6
L2_EPS = 1e-6
BF16_SUBLANES = 16
F32_SUBLANES = 8
VMEM_LIMIT_BYTES = 56 * 1024 * 1024

INPROJ_ROWS = 512
PREP_ROWS = 512
OUT_ROWS = 256


def _dot(a, b):
    return jnp.dot(a, b, preferred_element_type=F32)


def _dot_nt(a, b):
    return lax.dot_general(a, b, (((1,), (1,)), ((), ())), preferred_element_type=F32)


def _silu(x):
    return x * jax.nn.sigmoid(x)


def _ada_kernel(c_ref, w_ref, b_ref, o_ref):
    sc = _silu(c_ref[...])
    o_ref[...] = jnp.dot(sc, w_ref[...], preferred_element_type=F32,
                         precision=lax.Precision.HIGHEST) + b_ref[...]


def _ada(c_pad, w, b):
    m, d = c_pad.shape
    n = w.shape[1]
    bn = d
    return pl.pallas_call(
        _ada_kernel,
        grid=(n // bn,),
        in_specs=[pl.BlockSpec((m, d), lambda j: (0, 0)),
                  pl.BlockSpec((d, bn), lambda j: (0, j)),
                  pl.BlockSpec((1, bn), lambda j: (0, j))],
        out_specs=pl.BlockSpec((m, bn), lambda j: (0, j)),
        out_shape=jax.ShapeDtypeStruct((m, n), F32),
        name="ada",
    )(c_pad, w, b)


def _seg_cumsum(x, axis, reverse):
    n = x.shape[axis]
    pos = lax.broadcasted_iota(jnp.int32, x.shape, axis) % CHUNK
    s = 1
    while s < CHUNK:
        if reverse:
            x = x + jnp.where(pos < CHUNK - s, pltpu.roll(x, n - s, axis), 0.0)
        else:
            x = x + jnp.where(pos >= s, pltpu.roll(x, s, axis), 0.0)
        s *= 2
    return x


def _decay_and_beta(ab, alog, dtb, axis, nh):
    pre = ab + dtb
    softplus = jnp.maximum(pre, 0.0) + jnp.log(1.0 + jnp.exp(-jnp.abs(pre)))
    g = -jnp.exp(alog) * softplus
    beta = jax.nn.sigmoid(ab)
    col = lax.broadcasted_iota(jnp.int32, ab.shape, 1 - axis)
    gf = _seg_cumsum(g, axis, False)
    gb = _seg_cumsum(g, axis, True)
    return jnp.where(col < nh, gf, jnp.where(col < N_DIR * nh, gb, beta))


def _inproj_kernel(x_ref, sc_ref, sh_ref, nw_ref, wbg, wcg, wax, waz, wqkv, wz, wga, wgb, wab, wabt,
                   alog, dtb, alogt, dtbt,
                   p_ref, gz_ref, qkv_ref, zs_ref, sga_ref, sgb_ref, gcs_ref, gcst_ref, *, nh, cw):
    x = x_ref[...]
    h = x * lax.rsqrt(jnp.mean(x * x, axis=-1, keepdims=True) + NORM_EPS) * nw_ref[...]
    h = h * (1.0 + sc_ref[0]) + sh_ref[0]
    hb = h.astype(BF16)

    d_conv = p_ref.shape[1]
    for c0 in range(0, d_conv, cw):
        sl = slice(c0, c0 + cw)
        p_ref[:, sl] = (_dot(hb, wbg[:, sl]) * _dot(hb, wax[:, sl])).astype(BF16)
        gz_ref[:, sl] = (_dot(hb, wcg[:, sl]) * _silu(_dot(hb, waz[:, sl]))).astype(BF16)
    for c0 in range(0, qkv_ref.shape[1], cw):
        sl = slice(c0, c0 + cw)
        qkv_ref[:, sl] = _dot(hb, wqkv[:, sl]).astype(BF16)
    for c0 in range(0, zs_ref.shape[1], cw):
        sl = slice(c0, c0 + cw)
        zs_ref[:, sl] = _silu(_dot(hb, wz[:, sl])).astype(BF16)
    for c0 in range(0, sga_ref.shape[1], cw):
        sl = slice(c0, c0 + cw)
        sga_ref[:, sl] = jax.nn.sigmoid(_dot(hb, wga[:, sl])).astype(BF16)
        sgb_ref[:, sl] = jax.nn.sigmoid(_dot(hb, wgb[:, sl])).astype(BF16)

    gcs_ref[...] = _decay_and_beta(_dot(hb, wab[...]), alog[...], dtb[...], 0, nh)
    gcst_ref[...] = _decay_and_beta(_dot_nt(wabt[...], hb), alogt[...], dtbt[...], 1, nh)


def _inproj(x2, scale, shift, norm_w, ws, alog, dtb, *, seq, nh):
    rows, d = x2.shape
    tm = INPROJ_ROWS
    assert seq % tm == 0 and tm % CHUNK == 0
    tiles_per_seq = seq // tm
    wbg, wcg, wax, waz, wqkv, wz, wga, wgb, wab = ws
    nab = wab.shape[1]
    wabt = wab.T
    whole = pl.BlockSpec(memory_space=pltpu.VMEM)
    row_blk = lambda w: pl.BlockSpec((tm, w), lambda i: (i, 0))
    mod_blk = pl.BlockSpec((1, 1, d), lambda i: (i // tiles_per_seq, 0, 0))
    out_shapes = [jax.ShapeDtypeStruct((rows, w.shape[1]), BF16) for w in (wbg, wcg, wqkv, wz, wga, wgb)]
    out_shapes += [jax.ShapeDtypeStruct((rows, nab), F32), jax.ShapeDtypeStruct((nab, rows), F32)]
    out_specs = [row_blk(s.shape[1]) for s in out_shapes[:-1]]
    out_specs += [pl.BlockSpec((nab, tm), lambda i: (0, i))]
    return pl.pallas_call(
        functools.partial(_inproj_kernel, nh=nh, cw=512),
        grid=(rows // tm,),
        in_specs=[row_blk(d), mod_blk, mod_blk, whole] + [whole] * 10 + [whole] * 4,
        out_specs=out_specs,
        out_shape=out_shapes,
        compiler_params=pltpu.CompilerParams(dimension_semantics=("arbitrary",),
                                             vmem_limit_bytes=VMEM_LIMIT_BYTES),
        name="inproj",
    )(x2, scale, shift, norm_w, wbg, wcg, wax, waz, wqkv, wz, wga, wgb, wab, wabt,
      alog, dtb, alog.T, dtb.T)


def _conv_rows(ext_ref, xm, prev, nxt, w_ref, width):
    r = xm.shape[0]
    half = width // 2
    pad = F32_SUBLANES
    ext_ref[pad:pad + r, :] = xm
    ext_ref[pad - half:pad, :] = prev[prev.shape[0] - half:, :]
    ext_ref[pad + r:pad + r + half, :] = nxt[:half, :]
    acc = None
    for t in range(width):
        term = ext_ref[pad - half + t:pad - half + t + r, :] * w_ref[t:t + 1, :]
        acc = term if acc is None else acc + term
    return acc


def _halo_specs(tile_rows, width_blk, col_of, row_tile_of, total_rows):
    per = tile_rows // BF16_SUBLANES
    last = total_rows // BF16_SUBLANES - 1

    def prev_map(*g):
        return (jnp.maximum(row_tile_of(*g) * per - 1, 0), col_of(*g))

    def next_map(*g):
        return (jnp.minimum((row_tile_of(*g) + 1) * per, last), col_of(*g))

    return (pl.BlockSpec((BF16_SUBLANES, width_blk), prev_map),
            pl.BlockSpec((BF16_SUBLANES, width_blk), next_map))


GROUP = 256


def _half_rows(a, b, second):
    start = b if second else 0
    return jnp.concatenate([a[k + start:k + start + b] for k in range(0, a.shape[0], 2 * b)], axis=0)


def _spread_rows(c, b, second, other):
    parts = []
    for k in range(0, c.shape[0], b):
        pair = (other[k:k + b], c[k:k + b])
        parts.extend(pair if second else pair[::-1])
    return jnp.concatenate(parts, axis=0)


def _block_inverses(pmats, lowers, ii, jj):
    same_block = lambda s: (ii >> s) == (jj >> s)
    eye = jnp.where(ii == jj, 1.0, 0.0)
    xs = [eye + jnp.where(same_block(1), p, 0.0) for p in pmats]
    s = 1
    while (1 << s) < CHUNK:
        b = 1 << s
        off = same_block(s + 1) & jnp.logical_not(same_block(s))
        offs = [jnp.where(off, p, 0.0) for p in pmats]
        xbs = [x.astype(BF16) for x in xs]
        if b < F32_SUBLANES:
            ys = [_dot(o.astype(BF16), xb) for o, xb in zip(offs, xbs)]
            xs = [x + _dot(xb, y.astype(BF16)) for x, xb, y in zip(xs, xbs, ys)]
        else:
            zeros = jnp.zeros((GROUP // 2, GROUP), F32)
            ys = [_dot(_half_rows(o, b, lo).astype(BF16), xb) for o, xb, lo in zip(offs, xbs, lowers)]
            ups = [_dot(_half_rows(x, b, lo).astype(BF16), _spread_rows(y, b, lo, zeros).astype(BF16))
                   for x, y, lo in zip(xs, ys, lowers)]
            xs = [_spread_rows(_half_rows(x, b, lo) + up, b, lo, _half_rows(x, b, not lo))
                  for x, up, lo in zip(xs, ups, lowers)]
        s += 1
    return xs


def _prep_kernel(qm, qp, qn, km, kp, kn, vm, vp, vn, cwq, cwk, cwv, gcs_ref, gcst_ref,
                 lhs1_ref, lhs2_ref, u_ref, gl_ref, ext_ref, *, nrb, nh):
    r = pl.program_id(1)
    h = pl.program_id(2)
    first = r == 0
    last = r == nrb - 1

    def conv_silu(i, m, p, n, cw):
        pv = jnp.where(first, 0.0, p[...].astype(F32))
        nx = jnp.where(last, 0.0, n[...].astype(F32))
        return _silu(_conv_rows(ext_ref.at[i], m[...].astype(F32), pv, nx, cw, CONV_QKV_WIDTH))

    q = conv_silu(0, qm, qp, qn, cwq)
    k = conv_silu(1, km, kp, kn, cwk)
    v = conv_silu(2, vm, vp, vn, cwv)
    q = q * lax.rsqrt(jnp.sum(q * q, axis=-1, keepdims=True) + L2_EPS) * (HEAD_DIM ** -0.5)
    k = k * lax.rsqrt(jnp.sum(k * k, axis=-1, keepdims=True) + L2_EPS)

    gcs = gcs_ref[...]
    lane = lax.broadcasted_iota(jnp.int32, gcs.shape, 1)

    def pick(col):
        return jnp.sum(jnp.where(lane == col, gcs, 0.0), axis=1, keepdims=True)

    g_col = (pick(h), pick(nh + h))
    b_col = (pick(2 * nh + h), pick(3 * nh + h))
    g_row = (gcst_ref[pl.ds(h, 1), :], gcst_ref[pl.ds(nh + h, 1), :])

    ii = lax.broadcasted_iota(jnp.int32, (GROUP, GROUP), 0)
    jj = lax.broadcasted_iota(jnp.int32, (GROUP, GROUP), 1)
    same_chunk = (ii // CHUNK) == (jj // CHUNK)
    cpg = GROUP // CHUNK

    inst = []
    for gi in range(q.shape[0] // GROUP):
        rs = slice(gi * GROUP, (gi + 1) * GROUP)
        kc, qc, vc = k[rs], q[rs], v[rs]
        kq = _dot_nt(jnp.concatenate([kc, qc], axis=0).astype(BF16), kc.astype(BF16))
        kk, qk = kq[:GROUP], kq[GROUP:]
        for d in range(N_DIR):
            gc = g_col[d][rs]
            gr = g_row[d][:, rs]
            bc = b_col[d][rs]
            incl = same_chunk & ((ii >= jj) if d == 0 else (ii <= jj))
            strict = same_chunk & ((ii > jj) if d == 0 else (ii < jj))
            dm = jnp.where(incl, jnp.exp(jnp.where(incl, gc - gr, 0.0)), 0.0)
            pmat = jnp.where(strict, -(bc * kk * dm), 0.0)
            inst.append(dict(gi=gi, d=d, kc=kc, qc=qc, vc=vc, gc=gc, bc=bc, pmat=pmat, aqk=qk * dm))

    tmats = _block_inverses([it["pmat"] for it in inst], [it["d"] == 0 for it in inst], ii, jj)

    for it, tmat in zip(inst, tmats):
        gi, d, kc, qc, vc, gc, bc, aqk = (it[n] for n in ("gi", "d", "kc", "qc", "vc", "gc", "bc", "aqk"))
        eg = jnp.exp(gc)
        rhs = jnp.concatenate([vc * bc, kc * (bc * eg)], axis=1).astype(BF16)
        uw = _dot(tmat.astype(BF16), rhs)
        edge = CHUNK - 1 if d == 0 else 0
        g_tot = jnp.concatenate(
            [jnp.broadcast_to(gc[c * CHUNK + edge:c * CHUNK + edge + 1], (CHUNK, 1)) for c in range(cpg)],
            axis=0)
        kdt = (kc * jnp.exp(g_tot - gc)).T
        qd = qc * eg
        for c in range(cpg):
            rs = slice(c * CHUNK, (c + 1) * CHUNK)
            cc = gi * cpg + c
            lhs1_ref[d, cc] = jnp.concatenate([uw[rs, HEAD_DIM:], qd[rs]], axis=0).astype(BF16)
            lhs2_ref[d, cc] = jnp.concatenate([aqk[rs, rs], kdt[:, rs]], axis=0).astype(BF16)
            u_ref[d, cc] = uw[rs, :HEAD_DIM].astype(BF16)
            gl_ref[d, cc] = jnp.broadcast_to(jnp.exp(g_tot[c * CHUNK:c * CHUNK + 1]), (1, HEAD_DIM))


def _prep(qkv, conv_w, gcs, gcst, *, bsz, seq, nh):
    rows = qkv.shape[0]
    rr = PREP_ROWS
    nrb = seq // rr
    ncb = rr // CHUNK
    nc = seq // CHUNK
    ngc = gcs.shape[1]
    row_of = lambda b, r, h: b * nrb + r
    in_specs = []
    for part in range(3):
        col_of = functools.partial(lambda b, r, h, part: part * nh + h, part=part)
        in_specs.append(pl.BlockSpec((rr, HEAD_DIM), functools.partial(
            lambda b, r, h, part: (b * nrb + r, part * nh + h), part=part)))
        in_specs.extend(_halo_specs(rr, HEAD_DIM, col_of, row_of, rows))
    for part in range(3):
        in_specs.append(pl.BlockSpec((CONV_QKV_WIDTH, HEAD_DIM), functools.partial(
            lambda b, r, h, part: (0, part * nh + h), part=part)))
    in_specs.append(pl.BlockSpec((rr, ngc), lambda b, r, h: (b * nrb + r, 0)))
    in_specs.append(pl.BlockSpec((ngc, rr), lambda b, r, h: (0, b * nrb + r)))

    def out(shape_tail, dtype):
        blk = (None, N_DIR, None, ncb) + shape_tail
        zeros = (0,) * len(shape_tail)
        return (jax.ShapeDtypeStruct((bsz, N_DIR, nh, nc) + shape_tail, dtype),
                pl.BlockSpec(blk, lambda b, r, h: (b, 0, h, r) + zeros))

    outs = [out((2 * CHUNK, HEAD_DIM), BF16), out((CHUNK + HEAD_DIM, CHUNK), BF16),
            out((CHUNK, HEAD_DIM), BF16), out((1, HEAD_DIM), F32)]
    return pl.pallas_call(
        functools.partial(_prep_kernel, nrb=nrb, nh=nh),
        grid=(bsz, nrb, nh),
        in_specs=in_specs,
        out_specs=[o[1] for o in outs],
        out_shape=[o[0] for o in outs],
        scratch_shapes=[pltpu.VMEM((3, rr + 2 * F32_SUBLANES, HEAD_DIM), F32)],
        compiler_params=pltpu.CompilerParams(dimension_semantics=("arbitrary",) * 3,
                                             vmem_limit_bytes=VMEM_LIMIT_BYTES),
        name="prep",
    )(qkv, qkv, qkv, qkv, qkv, qkv, qkv, qkv, qkv, conv_w, conv_w, conv_w, gcs, gcst)


def _scan_kernel(l1f, l2f, uf, glf, l1b, l2b, ub, glb, of_ref, ob_ref, s_ref, *, ncb, nh):
    @pl.when(pl.program_id(1) == 0)
    def _():
        s_ref[...] = jnp.zeros_like(s_ref)

    dirs = ((l1f, l2f, uf, glf, of_ref), (l1b, l2b, ub, glb, ob_ref))

    def body(ci, carry):
        chains = [(d, h, ci if d == 0 else ncb - 1 - ci) for d in range(N_DIR) for h in range(nh)]
        states = [s_ref[d, h] for d, h, _ in chains]
        r1s = [_dot(dirs[d][0][h, cc], s.astype(BF16))
               for (d, h, cc), s in zip(chains, states)]
        v_news = [dirs[d][2][h, cc].astype(F32) - r1[:CHUNK] for (d, h, cc), r1 in zip(chains, r1s)]
        r2s = [_dot(dirs[d][1][h, cc], v.astype(BF16))
               for (d, h, cc), v in zip(chains, v_news)]
        for (d, h, cc), s, r1, r2 in zip(chains, states, r1s, r2s):
            s_ref[d, h] = s * dirs[d][3][h, cc] + r2[CHUNK:]
            r0 = pl.multiple_of(cc * CHUNK, CHUNK)
            dirs[d][4][pl.ds(r0, CHUNK), h * HEAD_DIM:(h + 1) * HEAD_DIM] = r1[CHUNK:] + r2[:CHUNK]
        return carry

    lax.fori_loop(0, ncb, body, 0)


def _scan(lhs1, lhs2, u, gl, *, seq):
    bsz, _, nh, nc = lhs1.shape[:4]
    rr = PREP_ROWS
    ncb = rr // CHUNK
    nt = nc // ncb

    def specs(d):
        def mk(arr):
            tail = arr.shape[4:]
            zeros = (0,) * len(tail)
            if d == 0:
                imap = lambda b, t: (b, 0, 0, t) + zeros
            else:
                imap = lambda b, t: (b, 1, 0, nt - 1 - t) + zeros
            return pl.BlockSpec((None, None, nh, ncb) + tail, imap)
        return [mk(a) for a in (lhs1, lhs2, u, gl)]

    o_shape = jax.ShapeDtypeStruct((bsz * seq, nh * HEAD_DIM), F32)
    return pl.pallas_call(
        functools.partial(_scan_kernel, ncb=ncb, nh=nh),
        grid=(bsz, nt),
        in_specs=specs(0) + specs(1),
        out_specs=[pl.BlockSpec((rr, nh * HEAD_DIM), lambda b, t: (b * nt + t, 0)),
                   pl.BlockSpec((rr, nh * HEAD_DIM), lambda b, t: (b * nt + nt - 1 - t, 0))],
        out_shape=[o_shape, o_shape],
        scratch_shapes=[pltpu.VMEM((N_DIR, nh, HEAD_DIM, HEAD_DIM), F32)],
        compiler_params=pltpu.CompilerParams(dimension_semantics=("arbitrary", "arbitrary"),
                                             vmem_limit_bytes=VMEM_LIMIT_BYTES),
        name="scan",
    )(lhs1, lhs2, u, gl, lhs1, lhs2, u, gl)


def _out_kernel(x_ref, of_ref, ob_ref, zs_ref, pm, pp, pn, gz_ref, sga_ref, sgb_ref, gate_ref,
                cwa_ref, gnw_ref, fnw_ref, wpa, wpb, wo, y_ref, ext_ref, *, tiles_per_seq, nh, final):
    r = pl.program_id(0) % tiles_per_seq
    o = of_ref[...] + ob_ref[...]
    heads = []
    for h in range(nh):
        oh = o[:, h * HEAD_DIM:(h + 1) * HEAD_DIM]
        heads.append(oh * lax.rsqrt(jnp.mean(oh * oh, axis=-1, keepdims=True) + NORM_EPS) * gnw_ref[...])
    yb_in = jnp.concatenate(heads, axis=1) * zs_ref[...].astype(F32)
    yb = _dot(yb_in.astype(BF16), wpb[...])

    pv = jnp.where(r == 0, 0.0, pp[...].astype(F32))
    nx = jnp.where(r == tiles_per_seq - 1, 0.0, pn[...].astype(F32))
    conv = _conv_rows(ext_ref, pm[...].astype(F32), pv, nx, cwa_ref, CONV_A_WIDTH)
    ya = _dot((gz_ref[...].astype(F32) * conv).astype(BF16), wpa[...])

    merged = sga_ref[...].astype(F32) * ya + sgb_ref[...].astype(F32) * yb
    xn = x_ref[...] + gate_ref[0] * _dot(merged.astype(BF16), wo[...])
    if final:
        xn = xn * lax.rsqrt(jnp.mean(xn * xn, axis=-1, keepdims=True) + NORM_EPS) * fnw_ref[...]
    y_ref[...] = xn


def _out(x2, o_f, o_b, zs, p, gz, sga, sgb, gate, conv_a_w, gdn_norm_w, final_norm_w, wpa, wpb, wo,
         *, seq, nh, final):
    rows, d = x2.shape
    tm = OUT_ROWS
    tiles_per_seq = seq // tm
    row_blk = lambda w: pl.BlockSpec((tm, w), lambda i: (i, 0))
    const = lambda a: pl.BlockSpec(a.shape, lambda i: (0,) * a.ndim)
    halo_prev, halo_next = _halo_specs(tm, p.shape[1], lambda i: 0, lambda i: i, rows)
    return pl.pallas_call(
        functools.partial(_out_kernel, tiles_per_seq=tiles_per_seq, nh=nh, final=final),
        grid=(rows // tm,),
        in_specs=[row_blk(d), row_blk(o_f.shape[1]), row_blk(o_b.shape[1]), row_blk(zs.shape[1]),
                  row_blk(p.shape[1]), halo_prev, halo_next, row_blk(gz.shape[1]),
                  row_blk(sga.shape[1]), row_blk(sgb.shape[1]),
                  pl.BlockSpec((1, 1, d), lambda i: (i // tiles_per_seq, 0, 0)),
                  const(conv_a_w), const(gdn_norm_w), const(final_norm_w),
                  const(wpa), const(wpb), const(wo)],
        out_specs=row_blk(d),
        out_shape=jax.ShapeDtypeStruct((rows, d), F32),
        scratch_shapes=[pltpu.VMEM((tm + 2 * F32_SUBLANES, p.shape[1]), F32)],
        compiler_params=pltpu.CompilerParams(dimension_semantics=("arbitrary",),
                                             vmem_limit_bytes=VMEM_LIMIT_BYTES),
        name="out",
    )(x2, o_f, o_b, zs, p, p, p, gz, sga, sgb, gate, conv_a_w, gdn_norm_w, final_norm_w, wpa, wpb, wo)


def kernel(x, c, w_ada, b_ada, norm_w, w_in, conv_a_w, conv_qkv_w, a_log, dt_bias, gdn_norm_w,
           w_pa, w_pb, w_o, final_norm_w):
    bsz, seq, d = x.shape
    depth = w_ada.shape[0]
    nh = a_log.shape[2]
    d_conv = conv_a_w.shape[2]
    key_dim = nh * HEAD_DIM
    assert conv_qkv_w.shape[2] == 3 * key_dim and gdn_norm_w.shape[1] == HEAD_DIM
    assert seq % PREP_ROWS == 0 and seq % OUT_ROWS == 0
    n_gate = N_DIR * nh
    splits = (d_conv,) * 4 + (key_dim,) * 4 + (n_gate, n_gate, d, d)
    assert sum(splits) == w_in.shape[2]
    offs = [0]
    for s in splits:
        offs.append(offs[-1] + s)

    x2 = x.reshape(bsz * seq, d)
    c_pad = jnp.zeros((8, d), F32).at[:bsz].set(c)
    for l in range(depth):
        mod = _ada(c_pad, w_ada[l], b_ada[l][None, :])[:bsz]
        shift, scale, gate = (mod[:, i * d:(i + 1) * d][:, None, :] for i in range(3))

        wb = w_in[l].astype(BF16)
        col = lambda a, b: wb[:, offs[a]:offs[b]]
        ws = (col(0, 1), col(1, 2), col(2, 3), col(3, 4), col(4, 7), col(7, 8), col(10, 11), col(11, 12),
              col(8, 10))
        zeros = jnp.zeros((1, n_gate), F32)
        alog = jnp.concatenate([a_log[l].reshape(1, n_gate), zeros], axis=1)
        dtb = jnp.concatenate([dt_bias[l].reshape(1, n_gate), zeros], axis=1)
        p, gz, qkv, zs, sga, sgb, gcs, gcst = _inproj(
            x2, scale, shift, norm_w[l][None, :], ws, alog, dtb, seq=seq, nh=nh)

        lhs1, lhs2, u, gl = _prep(qkv, conv_qkv_w[l], gcs, gcst, bsz=bsz, seq=seq, nh=nh)
        o_f, o_b = _scan(lhs1, lhs2, u, gl, seq=seq)

        x2 = _out(x2, o_f, o_b, zs, p, gz, sga, sgb, gate, conv_a_w[l], gdn_norm_w[l][None, :],
                  final_norm_w[None, :], w_pa[l].astype(BF16), w_pb[l].astype(BF16), w_o[l].astype(BF16),
                  seq=seq, nh=nh, final=(l == depth - 1))
    return x2.reshape(bsz, seq, d)
```

```python
import functools

import jax
import jax.numpy as jnp
from jax import lax
from jax.experimental import pallas as pl
from jax.experimental.pallas import tpu as pltpu

F32 = jnp.float32
BF16 = jnp.bfloat16

HEAD_DIM = 128
CHUNK = 64
N_DIR = 2
CONV_A_WIDTH = 3
CONV_QKV_WIDTH = 5
NORM_EPS = 1e-6
L2_EPS = 1e-6
LOG2_E = 1.4426950408889634
BF16_SUBLANES = 16
F32_SUBLANES = 8
VMEM_LIMIT_BYTES = 56 * 1024 * 1024

INPROJ_ROWS = 512
PREP_ROWS = 1024
SCAN_ROWS = 512
OUT_ROWS = 256


def _dot(a, b):
    return jnp.dot(a, b, preferred_element_type=F32)


def _dot_nt(a, b):
    return lax.dot_general(a, b, (((1,), (1,)), ((), ())), preferred_element_type=F32)


def _silu(x):
    return x * jax.nn.sigmoid(x)


def _ada_kernel(c_ref, w_ref, b_ref, o_ref):
    sc = _silu(c_ref[...])
    o_ref[...] = jnp.dot(sc, w_ref[...], preferred_element_type=F32,
                         precision=lax.Precision.HIGHEST) + b_ref[...]


def _ada(c_pad, w, b):
    m, d = c_pad.shape
    n = w.shape[1]
    bn = d
    return pl.pallas_call(
        _ada_kernel,
        grid=(n // bn,),
        in_specs=[pl.BlockSpec((m, d), lambda j: (0, 0)),
                  pl.BlockSpec((d, bn), lambda j: (0, j)),
                  pl.BlockSpec((1, bn), lambda j: (0, j))],
        out_specs=pl.BlockSpec((m, bn), lambda j: (0, j)),
        out_shape=jax.ShapeDtypeStruct((m, n), F32),
        name="ada",
    )(c_pad, w, b)


def _seg_cumsum(x, axis, reverse):
    n = x.shape[axis]
    pos = lax.broadcasted_iota(jnp.int32, x.shape, axis) % CHUNK
    s = 1
    while s < CHUNK:
        if reverse:
            x = x + jnp.where(pos < CHUNK - s, pltpu.roll(x, n - s, axis), 0.0)
        else:
            x = x + jnp.where(pos >= s, pltpu.roll(x, s, axis), 0.0)
        s *= 2
    return x


def _decay_and_beta(ab, alog, dtb, axis, nh):
    pre = ab + dtb
    softplus = jnp.maximum(pre, 0.0) + jnp.log(1.0 + jnp.exp(-jnp.abs(pre)))
    g = -jnp.exp(alog) * softplus * LOG2_E
    beta = jax.nn.sigmoid(ab)
    col = lax.broadcasted_iota(jnp.int32, ab.shape, 1 - axis)
    gf = _seg_cumsum(g, axis, False)
    gb = _seg_cumsum(g, axis, True)
    return jnp.where(col < nh, gf, jnp.where(col < N_DIR * nh, gb, beta))


def _inproj_kernel(x_ref, sc_ref, sh_ref, nw_ref, wbg, wcg, wax, waz, wqkv, wz, wga, wgb, wab, wabt,
                   alog, dtb, alogt, dtbt,
                   p_ref, gz_ref, qkv_ref, zs_ref, sga_ref, sgb_ref, gcs_ref, gcst_ref, *, nh, cw):
    x = x_ref[...]
    h = x * lax.rsqrt(jnp.mean(x * x, axis=-1, keepdims=True) + NORM_EPS) * nw_ref[...]
    h = h * (1.0 + sc_ref[0]) + sh_ref[0]
    hb = h.astype(BF16)

    d_conv = p_ref.shape[1]
    for c0 in range(0, d_conv, cw):
        sl = slice(c0, c0 + cw)
        p_ref[:, sl] = (_dot(hb, wbg[:, sl]) * _dot(hb, wax[:, sl])).astype(BF16)
        gz_ref[:, sl] = (_dot(hb, wcg[:, sl]) * _silu(_dot(hb, waz[:, sl]))).astype(BF16)
    for c0 in range(0, qkv_ref.shape[1], cw):
        sl = slice(c0, c0 + cw)
        qkv_ref[:, sl] = _dot(hb, wqkv[:, sl]).astype(BF16)
    for c0 in range(0, zs_ref.shape[1], cw):
        sl = slice(c0, c0 + cw)
        zs_ref[:, sl] = _silu(_dot(hb, wz[:, sl])).astype(BF16)
    for c0 in range(0, sga_ref.shape[1], cw):
        sl = slice(c0, c0 + cw)
        sga_ref[:, sl] = jax.nn.sigmoid(_dot(hb, wga[:, sl])).astype(BF16)
        sgb_ref[:, sl] = jax.nn.sigmoid(_dot(hb, wgb[:, sl])).astype(BF16)

    gcs_ref[...] = _decay_and_beta(_dot(hb, wab[...]), alog[...], dtb[...], 0, nh)
    gcst_ref[...] = _decay_and_beta(_dot_nt(wabt[...], hb), alogt[...], dtbt[...], 1, nh)


def _inproj(x2, scale, shift, norm_w, ws, alog, dtb, *, seq, nh):
    rows, d = x2.shape
    tm = INPROJ_ROWS
    assert seq % tm == 0 and tm % CHUNK == 0
    tiles_per_seq = seq // tm
    wbg, wcg, wax, waz, wqkv, wz, wga, wgb, wab = ws
    nab = wab.shape[1]
    wabt = wab.T
    resident = lambda a: pl.BlockSpec(a.shape, lambda i: (0,) * a.ndim, pipeline_mode=pl.Buffered(1))
    consts = (norm_w, wbg, wcg, wax, waz, wqkv, wz, wga, wgb, wab, wabt, alog, dtb, alog.T, dtb.T)
    row_blk = lambda w: pl.BlockSpec((tm, w), lambda i: (i, 0))
    mod_blk = pl.BlockSpec((1, 1, d), lambda i: (i // tiles_per_seq, 0, 0))
    out_shapes = [jax.ShapeDtypeStruct((rows, w.shape[1]), BF16) for w in (wbg, wcg, wqkv, wz, wga, wgb)]
    out_shapes += [jax.ShapeDtypeStruct((rows, nab), F32), jax.ShapeDtypeStruct((nab, rows), F32)]
    out_specs = [row_blk(s.shape[1]) for s in out_shapes[:-1]]
    out_specs += [pl.BlockSpec((nab, tm), lambda i: (0, i))]
    return pl.pallas_call(
        functools.partial(_inproj_kernel, nh=nh, cw=512),
        grid=(rows // tm,),
        in_specs=[row_blk(d), mod_blk, mod_blk] + [resident(a) for a in consts],
        out_specs=out_specs,
        out_shape=out_shapes,
        compiler_params=pltpu.CompilerParams(dimension_semantics=("arbitrary",),
                                             vmem_limit_bytes=VMEM_LIMIT_BYTES),
        name="inproj",
    )(x2, scale, shift, *consts)


def _conv_rows(ext_ref, xm, prev, nxt, w_ref, width):
    r = xm.shape[0]
    half = width // 2
    pad = F32_SUBLANES
    ext_ref[pad:pad + r, :] = xm
    ext_ref[pad - half:pad, :] = prev[prev.shape[0] - half:, :]
    ext_ref[pad + r:pad + r + half, :] = nxt[:half, :]
    acc = None
    for t in range(width):
        term = ext_ref[pad - half + t:pad - half + t + r, :] * w_ref[t:t + 1, :]
        acc = term if acc is None else acc + term
    return acc


def _halo_specs(tile_rows, width_blk, col_of, row_tile_of, total_rows):
    per = tile_rows // BF16_SUBLANES
    last = total_rows // BF16_SUBLANES - 1

    def prev_map(*g):
        return (jnp.maximum(row_tile_of(*g) * per - 1, 0), col_of(*g))

    def next_map(*g):
        return (jnp.minimum((row_tile_of(*g) + 1) * per, last), col_of(*g))

    return (pl.BlockSpec((BF16_SUBLANES, width_blk), prev_map),
            pl.BlockSpec((BF16_SUBLANES, width_blk), next_map))


GROUP = 128


def _half_rows(a, b, second):
    start = b if second else 0
    return jnp.concatenate([a[k + start:k + start + b] for k in range(0, a.shape[0], 2 * b)], axis=0)


def _spread_rows(c, b, second, other):
    parts = []
    for k in range(0, c.shape[0], b):
        pair = (other[k:k + b], c[k:k + b])
        parts.extend(pair if second else pair[::-1])
    return jnp.concatenate(parts, axis=0)


def _zero_row_after(v):
    bits = pltpu.bitcast(v[:F32_SUBLANES], jnp.int32)
    zero = lax.shift_right_logical(lax.shift_right_logical(bits, 16), 16)
    return zero[:1].astype(F32)


def _block_inverses(pmats, lowers, ii, jj, fillers=()):
    same_block = lambda s: (ii >> s) == (jj >> s)
    eye = jnp.where(ii == jj, 1.0, 0.0)
    xs = [eye + jnp.where(same_block(1), p, 0.0) for p in pmats]
    s = 1
    while (1 << s) < CHUNK:
        b = 1 << s
        off = same_block(s + 1) & jnp.logical_not(same_block(s))
        offs = [jnp.where(off, p, 0.0) for p in pmats]
        xbs = [x.astype(BF16) for x in xs]
        if b < F32_SUBLANES:
            ys = [_dot(o.astype(BF16), xb) for o, xb in zip(offs, xbs)]
            xs = [x + _dot(xb, y.astype(BF16)) for x, xb, y in zip(xs, xbs, ys)]
        else:
            zeros = jnp.zeros((GROUP // 2, GROUP), F32)
            ys = [_dot(_half_rows(o, b, lo).astype(BF16), xb) for o, xb, lo in zip(offs, xbs, lowers)]
            ups = [_dot(_half_rows(x, b, lo).astype(BF16), _spread_rows(y, b, lo, zeros).astype(BF16))
                   for x, y, lo in zip(xs, ys, lowers)]
            xs = [_spread_rows(_half_rows(x, b, lo) + up, b, lo, _half_rows(x, b, not lo))
                  for x, up, lo in zip(xs, ups, lowers)]
        if s - 1 < len(fillers):
            fillers[s - 1](_zero_row_after(ys[-1]))
        s += 1
    assert len(fillers) < s
    return xs


def _prep_kernel(qm, qp, qn, km, kp, kn, vm, vp, vn, cwq, cwk, cwv, gcs_ref, gcst_ref,
                 lhs1_ref, lhs2_ref, u_ref, gl_ref, ext_ref, qkv_s, col_s, row_s, *, nrb, nh):
    i = pl.program_id(0)
    j = jnp.minimum(i, pl.num_programs(0) - 2)
    r = (j // nh) % nrb
    h = j % nh
    first = r == 0
    last = r == nrb - 1

    @pl.when(i == 0)
    def _():
        qkv_s[...] = jnp.zeros_like(qkv_s)
        col_s[...] = jnp.zeros_like(col_s)
        row_s[...] = jnp.zeros_like(row_s)

    q, k, v = qkv_s[0], qkv_s[1], qkv_s[2]
    g_col = (col_s[0], col_s[1])
    b_col = (col_s[2], col_s[3])
    nbc = (-b_col[0], -b_col[1])
    g_row = (row_s[0], row_s[1])

    def conv_silu(idx, m, p, n, cw, zero_row):
        pv = jnp.where(first, 0.0, p[...].astype(F32))
        nx = jnp.where(last, 0.0, n[...].astype(F32))
        w = cw[...] + zero_row
        return _silu(_conv_rows(ext_ref.at[idx], m[...].astype(F32), pv, nx, w, CONV_QKV_WIDTH))

    def stage1_q(zero_row):
        q1 = conv_silu(0, qm, qp, qn, cwq, zero_row)
        qkv_s[0] = q1 * lax.rsqrt(jnp.sum(q1 * q1, axis=-1, keepdims=True) + L2_EPS) * (HEAD_DIM ** -0.5)

    def stage1_k(zero_row):
        k1 = conv_silu(1, km, kp, kn, cwk, zero_row)
        qkv_s[1] = k1 * lax.rsqrt(jnp.sum(k1 * k1, axis=-1, keepdims=True) + L2_EPS)

    def stage1_v(zero_row):
        qkv_s[2] = conv_silu(2, vm, vp, vn, cwv, zero_row)

    def stage1_gates(zero_row):
        del zero_row
        gcs = gcs_ref[...]
        lane = lax.broadcasted_iota(jnp.int32, gcs.shape, 1)
        for slot in range(4):
            picked = jnp.sum(jnp.where(lane == slot * nh + h, gcs, 0.0), axis=1, keepdims=True)
            col_s[slot] = jnp.broadcast_to(picked, col_s.shape[1:])
        for d in range(N_DIR):
            row_s[d] = gcst_ref[pl.ds(d * nh + h, 1), :]

    ii = lax.broadcasted_iota(jnp.int32, (GROUP, GROUP), 0)
    jj = lax.broadcasted_iota(jnp.int32, (GROUP, GROUP), 1)
    same_chunk = (ii // CHUNK) == (jj // CHUNK)
    cpg = GROUP // CHUNK

    inst = []
    for gi in range(q.shape[0] // GROUP):
        rs = slice(gi * GROUP, (gi + 1) * GROUP)
        kc, qc, vc = k[rs], q[rs], v[rs]
        kq = _dot_nt(jnp.concatenate([kc, qc], axis=0).astype(BF16), kc.astype(BF16))
        kk, qk = kq[:GROUP], kq[GROUP:]
        for d in range(N_DIR):
            gc = g_col[d][rs]
            gr = g_row[d][:, rs]
            bc = b_col[d][rs]
            incl = same_chunk & ((ii >= jj) if d == 0 else (ii <= jj))
            strict = same_chunk & ((ii > jj) if d == 0 else (ii < jj))
            decay = jnp.exp2(gc - gr)
            pmat = jnp.where(strict, (nbc[d][rs] * kk) * decay, 0.0)
            aqk = jnp.where(incl, qk * decay, 0.0)
            inst.append(dict(gi=gi, d=d, kc=kc, qc=qc, vc=vc, gc=gc, bc=bc, pmat=pmat, aqk=aqk))

    tmats = _block_inverses([it["pmat"] for it in inst], [it["d"] == 0 for it in inst], ii, jj,
                            fillers=[stage1_gates, stage1_q, stage1_k, stage1_v])

    for it, tmat in zip(inst, tmats):
        gi, d, kc, qc, vc, gc, bc, aqk = (it[n] for n in ("gi", "d", "kc", "qc", "vc", "gc", "bc", "aqk"))
        eg = jnp.exp2(gc)
        rhs = jnp.concatenate([vc * bc, kc * (bc * eg)], axis=1).astype(BF16)
        uw = _dot(tmat.astype(BF16), rhs)
        edge = CHUNK - 1 if d == 0 else 0
        g_tot = jnp.concatenate(
            [jnp.broadcast_to(gc[c * CHUNK + edge:c * CHUNK + edge + 1], (CHUNK, HEAD_DIM)) for c in range(cpg)],
            axis=0)
        kdt = (kc * jnp.exp2(g_tot - gc)).T
        qd = qc * eg
        for c in range(cpg):
            rs = slice(c * CHUNK, (c + 1) * CHUNK)
            cc = gi * cpg + c
            lhs1_ref[d, cc] = jnp.concatenate([uw[rs, HEAD_DIM:], qd[rs]], axis=0).astype(BF16)
            u_ref[d, cc] = uw[rs, :HEAD_DIM].astype(BF16)
            gl_ref[d, cc] = jnp.exp2(g_tot[c * CHUNK:c * CHUNK + 1])
        for p in range(cpg // 2):
            ls = slice(p * 2 * CHUNK, (p + 1) * 2 * CHUNK)
            a_pair = aqk[p * 2 * CHUNK:p * 2 * CHUNK + CHUNK, ls] + aqk[p * 2 * CHUNK + CHUNK:(p + 1) * 2 * CHUNK, ls]
            lhs2_ref[d, gi * (cpg // 2) + p] = jnp.concatenate([a_pair, kdt[:, ls]], axis=0).astype(BF16)


def _prep(qkv, conv_w, gcs, gcst, *, bsz, seq, nh):
    rows = qkv.shape[0]
    rr = PREP_ROWS
    nrb = seq // rr
    ncb = rr // CHUNK
    nc = seq // CHUNK
    ngc = gcs.shape[1]
    n_tiles = bsz * nrb * nh
    assert GROUP == HEAD_DIM and rr % GROUP == 0

    tile_in = lambda i: jnp.minimum(i, n_tiles - 1)
    tile_out = lambda i: jnp.maximum(i - 1, 0)
    row_of = lambda i: tile_in(i) // nh
    head_of = lambda i: tile_in(i) % nh
    in_specs = []
    for part in range(3):
        col_of = functools.partial(lambda i, part: part * nh + head_of(i), part=part)
        in_specs.append(pl.BlockSpec((rr, HEAD_DIM), functools.partial(
            lambda i, col_of: (row_of(i), col_of(i)), col_of=col_of)))
        in_specs.extend(_halo_specs(rr, HEAD_DIM, col_of, row_of, rows))
    for part in range(3):
        in_specs.append(pl.BlockSpec((CONV_QKV_WIDTH, HEAD_DIM), functools.partial(
            lambda i, part: (0, part * nh + head_of(i)), part=part)))
    in_specs.append(pl.BlockSpec((rr, ngc), lambda i: (row_of(i), 0)))
    in_specs.append(pl.BlockSpec((ngc, rr), lambda i: (0, row_of(i))))

    def out(per_block, shape_tail, dtype):
        blk = (None, N_DIR, None, per_block) + shape_tail
        zeros = (0,) * len(shape_tail)

        def imap(i):
            t = tile_out(i)
            return (t // (nrb * nh), 0, t % nh, (t // nh) % nrb) + zeros

        return (jax.ShapeDtypeStruct((bsz, N_DIR, nh, per_block * nrb) + shape_tail, dtype),
                pl.BlockSpec(blk, imap))

    outs = [out(ncb, (2 * CHUNK, HEAD_DIM), BF16), out(ncb // 2, (CHUNK + HEAD_DIM, 2 * CHUNK), BF16),
            out(ncb, (CHUNK, HEAD_DIM), BF16), out(ncb, (1, HEAD_DIM), F32)]
    return pl.pallas_call(
        functools.partial(_prep_kernel, nrb=nrb, nh=nh),
        grid=(n_tiles + 1,),
        in_specs=in_specs,
        out_specs=[o[1] for o in outs],
        out_shape=[o[0] for o in outs],
        scratch_shapes=[pltpu.VMEM((3, rr + 2 * F32_SUBLANES, HEAD_DIM), F32),
                        pltpu.VMEM((3, rr, HEAD_DIM), F32),
                        pltpu.VMEM((4, rr, HEAD_DIM), F32),
                        pltpu.VMEM((N_DIR, 1, rr), F32)],
        compiler_params=pltpu.CompilerParams(dimension_semantics=("arbitrary",),
                                             vmem_limit_bytes=VMEM_LIMIT_BYTES),
        name="prep",
    )(qkv, qkv, qkv, qkv, qkv, qkv, qkv, qkv, qkv, conv_w, conv_w, conv_w, gcs, gcst)


def _scan_kernel(l1f, l2f, uf, glf, l1b, l2b, ub, glb, of_ref, ob_ref, s_ref, *, ncb, nh):
    @pl.when(pl.program_id(1) == 0)
    def _():
        s_ref[...] = jnp.zeros_like(s_ref)

    dirs = ((l1f, l2f, uf, glf, of_ref), (l1b, l2b, ub, glb, ob_ref))

    zero_rows = jnp.zeros((CHUNK, HEAD_DIM), BF16)

    def chunk_step(pair, parity):
        chains = [(d, h, pair[d], 2 * pair[d] + parity[d]) for d in range(N_DIR) for h in range(nh)]
        states = [s_ref[d, h] for d, h, _, _ in chains]
        r1s = [_dot(dirs[d][0][h, cc], s.astype(BF16))
               for (d, h, _, cc), s in zip(chains, states)]
        v_news = [dirs[d][2][h, cc].astype(F32) - r1[:CHUNK] for (d, h, _, cc), r1 in zip(chains, r1s)]
        r2s = []
        for (d, h, pc, _), v in zip(chains, v_news):
            vb = v.astype(BF16)
            rhs = jnp.concatenate([zero_rows, vb] if parity[d] else [vb, zero_rows], axis=0)
            r2s.append(_dot(dirs[d][1][h, pc], rhs))
        for (d, h, _, cc), s, r1, r2 in zip(chains, states, r1s, r2s):
            s_ref[d, h] = s * dirs[d][3][h, cc] + r2[CHUNK:]
            r0 = pl.multiple_of(cc * CHUNK, CHUNK)
            dirs[d][4][pl.ds(r0, CHUNK), h * HEAD_DIM:(h + 1) * HEAD_DIM] = (
                r1[CHUNK:] + r2[:CHUNK]).astype(dirs[d][4].dtype)

    def body(pi, carry):
        pair = (pi, ncb // 2 - 1 - pi)
        chunk_step(pair, (0, 1))
        chunk_step(pair, (1, 0))
        return carry

    lax.fori_loop(0, ncb // 2, body, 0)


def _scan(lhs1, lhs2, u, gl, *, seq):
    bsz, _, nh, nc = lhs1.shape[:4]
    rr = SCAN_ROWS
    ncb = rr // CHUNK
    nt = nc // ncb

    def specs(d):
        def mk(arr):
            tail = arr.shape[4:]
            zeros = (0,) * len(tail)
            if d == 0:
                imap = lambda b, t: (b, 0, 0, t) + zeros
            else:
                imap = lambda b, t: (b, 1, 0, nt - 1 - t) + zeros
            return pl.BlockSpec((None, None, nh, arr.shape[3] // nt) + tail, imap)
        return [mk(a) for a in (lhs1, lhs2, u, gl)]

    o_shape = jax.ShapeDtypeStruct((bsz * seq, nh * HEAD_DIM), BF16)
    return pl.pallas_call(
        functools.partial(_scan_kernel, ncb=ncb, nh=nh),
        grid=(bsz, nt),
        in_specs=specs(0) + specs(1),
        out_specs=[pl.BlockSpec((rr, nh * HEAD_DIM), lambda b, t: (b * nt + t, 0)),
                   pl.BlockSpec((rr, nh * HEAD_DIM), lambda b, t: (b * nt + nt - 1 - t, 0))],
        out_shape=[o_shape, o_shape],
        scratch_shapes=[pltpu.VMEM((N_DIR, nh, HEAD_DIM, HEAD_DIM), F32)],
        compiler_params=pltpu.CompilerParams(dimension_semantics=("arbitrary", "arbitrary"),
                                             vmem_limit_bytes=VMEM_LIMIT_BYTES),
        name="scan",
    )(lhs1, lhs2, u, gl, lhs1, lhs2, u, gl)


def _out_kernel(x_ref, of_ref, ob_ref, zs_ref, pm, pp, pn, gz_ref, sga_ref, sgb_ref, gate_ref,
                cwa_ref, gnw_ref, fnw_ref, wpa, wpb, wo, y_ref, ext_ref, *, tiles_per_seq, nh, final):
    r = pl.program_id(0) % tiles_per_seq
    o = of_ref[...].astype(F32) + ob_ref[...].astype(F32)
    heads = []
    for h in range(nh):
        oh = o[:, h * HEAD_DIM:(h + 1) * HEAD_DIM]
        heads.append(oh * lax.rsqrt(jnp.mean(oh * oh, axis=-1, keepdims=True) + NORM_EPS) * gnw_ref[...])
    yb_in = jnp.concatenate(heads, axis=1) * zs_ref[...].astype(F32)
    yb = _dot(yb_in.astype(BF16), wpb[...])

    pv = jnp.where(r == 0, 0.0, pp[...].astype(F32))
    nx = jnp.where(r == tiles_per_seq - 1, 0.0, pn[...].astype(F32))
    conv = _conv_rows(ext_ref, pm[...].astype(F32), pv, nx, cwa_ref, CONV_A_WIDTH)
    ya = _dot((gz_ref[...].astype(F32) * conv).astype(BF16), wpa[...])

    merged = sga_ref[...].astype(F32) * ya + sgb_ref[...].astype(F32) * yb
    xn = x_ref[...] + gate_ref[0] * _dot(merged.astype(BF16), wo[...])
    if final:
        xn = xn * lax.rsqrt(jnp.mean(xn * xn, axis=-1, keepdims=True) + NORM_EPS) * fnw_ref[...]
    y_ref[...] = xn


def _out(x2, o_f, o_b, zs, p, gz, sga, sgb, gate, conv_a_w, gdn_norm_w, final_norm_w, wpa, wpb, wo,
         *, seq, nh, final):
    rows, d = x2.shape
    tm = OUT_ROWS
    tiles_per_seq = seq // tm
    row_blk = lambda w: pl.BlockSpec((tm, w), lambda i: (i, 0))
    const = lambda a: pl.BlockSpec(a.shape, lambda i: (0,) * a.ndim)
    halo_prev, halo_next = _halo_specs(tm, p.shape[1], lambda i: 0, lambda i: i, rows)
    return pl.pallas_call(
        functools.partial(_out_kernel, tiles_per_seq=tiles_per_seq, nh=nh, final=final),
        grid=(rows // tm,),
        in_specs=[row_blk(d), row_blk(o_f.shape[1]), row_blk(o_b.shape[1]), row_blk(zs.shape[1]),
                  row_blk(p.shape[1]), halo_prev, halo_next, row_blk(gz.shape[1]),
                  row_blk(sga.shape[1]), row_blk(sgb.shape[1]),
                  pl.BlockSpec((1, 1, d), lambda i: (i // tiles_per_seq, 0, 0)),
                  const(conv_a_w), const(gdn_norm_w), const(final_norm_w),
                  const(wpa), const(wpb), const(wo)],
        out_specs=row_blk(d),
        out_shape=jax.ShapeDtypeStruct((rows, d), F32),
        scratch_shapes=[pltpu.VMEM((tm + 2 * F32_SUBLANES, p.shape[1]), F32)],
        compiler_params=pltpu.CompilerParams(dimension_semantics=("arbitrary",),
                                             vmem_limit_bytes=VMEM_LIMIT_BYTES),
        name="out",
    )(x2, o_f, o_b, zs, p, p, p, gz, sga, sgb, gate, conv_a_w, gdn_norm_w, final_norm_w, wpa, wpb, wo)


def kernel(x, c, w_ada, b_ada, norm_w, w_in, conv_a_w, conv_qkv_w, a_log, dt_bias, gdn_norm_w,
           w_pa, w_pb, w_o, final_norm_w):
    bsz, seq, d = x.shape
    depth = w_ada.shape[0]
    nh = a_log.shape[2]
    d_conv = conv_a_w.shape[2]
    key_dim = nh * HEAD_DIM
    assert conv_qkv_w.shape[2] == 3 * key_dim and gdn_norm_w.shape[1] == HEAD_DIM
    assert seq % PREP_ROWS == 0 and seq % SCAN_ROWS == 0 and seq % OUT_ROWS == 0
    n_gate = N_DIR * nh
    splits = (d_conv,) * 4 + (key_dim,) * 4 + (n_gate, n_gate, d, d)
    assert sum(splits) == w_in.shape[2]
    offs = [0]
    for s in splits:
        offs.append(offs[-1] + s)

    x2 = x.reshape(bsz * seq, d)
    c_pad = jnp.zeros((8, d), F32).at[:bsz].set(c)
    for l in range(depth):
        mod = _ada(c_pad, w_ada[l], b_ada[l][None, :])[:bsz]
        shift, scale, gate = (mod[:, i * d:(i + 1) * d][:, None, :] for i in range(3))

        wb = w_in[l].astype(BF16)
        col = lambda a, b: wb[:, offs[a]:offs[b]]
        ws = (col(0, 1), col(1, 2), col(2, 3), col(3, 4), col(4, 7), col(7, 8), col(10, 11), col(11, 12),
              col(8, 10))
        zeros = jnp.zeros((1, n_gate), F32)
        alog = jnp.concatenate([a_log[l].reshape(1, n_gate), zeros], axis=1)
        dtb = jnp.concatenate([dt_bias[l].reshape(1, n_gate), zeros], axis=1)
        p, gz, qkv, zs, sga, sgb, gcs, gcst = _inproj(
            x2, scale, shift, norm_w[l][None, :], ws, alog, dtb, seq=seq, nh=nh)

        lhs1, lhs2, u, gl = _prep(qkv, conv_qkv_w[l], gcs, gcst, bsz=bsz, seq=seq, nh=nh)
        o_f, o_b = _scan(lhs1, lhs2, u, gl, seq=seq)

        x2 = _out(x2, o_f, o_b, zs, p, gz, sga, sgb, gate, conv_a_w[l], gdn_norm_w[l][None, :],
                  final_norm_w[None, :], w_pa[l].astype(BF16), w_pb[l].astype(BF16), w_o[l].astype(BF16),
                  seq=seq, nh=nh, final=(l == depth - 1))
    return x2.reshape(bsz, seq, d)
```

```python
import functools

import jax
import jax.numpy as jnp
from jax import lax
from jax.experimental import pallas as pl
from jax.experimental.pallas import tpu as pltpu

F32 = jnp.float32
BF16 = jnp.bfloat16

HEAD_DIM = 128
CHUNK = 64
N_DIR = 2
CONV_A_WIDTH = 3
CONV_QKV_WIDTH = 5
NORM_EPS = 1e-6
L2_EPS = 1e-6
LOG2_E = 1.4426950408889634
BF16_SUBLANES = 16
F32_SUBLANES = 8
VMEM_LIMIT_BYTES = 56 * 1024 * 1024

INPROJ_ROWS = 512
INPROJ_COLS = 512
PREP_ROWS = 1024
SCAN_ROWS = 512
OUT_ROWS = 512
OUT_SUBTILES = 2


def _dot(a, b):
    return jnp.dot(a, b, preferred_element_type=F32)


def _dot_nt(a, b):
    return lax.dot_general(a, b, (((1,), (1,)), ((), ())), preferred_element_type=F32)


def _silu(x):
    return x * jax.nn.sigmoid(x)


def _ada_kernel(c_ref, w_ref, b_ref, o_ref):
    sc = _silu(c_ref[...])
    o_ref[...] = jnp.dot(sc, w_ref[...], preferred_element_type=F32,
                         precision=lax.Precision.HIGHEST) + b_ref[...]


def _ada(c_pad, w, b):
    m, d = c_pad.shape
    n = w.shape[1]
    bn = d
    return pl.pallas_call(
        _ada_kernel,
        grid=(n // bn,),
        in_specs=[pl.BlockSpec((m, d), lambda j: (0, 0)),
                  pl.BlockSpec((d, bn), lambda j: (0, j)),
                  pl.BlockSpec((1, bn), lambda j: (0, j))],
        out_specs=pl.BlockSpec((m, bn), lambda j: (0, j)),
        out_shape=jax.ShapeDtypeStruct((m, n), F32),
        name="ada",
    )(c_pad, w, b)


def _seg_cumsum(x, axis, reverse):
    n = x.shape[axis]
    pos = lax.broadcasted_iota(jnp.int32, x.shape, axis) % CHUNK
    s = 1
    while s < CHUNK:
        if reverse:
            x = x + jnp.where(pos < CHUNK - s, pltpu.roll(x, n - s, axis), 0.0)
        else:
            x = x + jnp.where(pos >= s, pltpu.roll(x, s, axis), 0.0)
        s *= 2
    return x


def _decay_and_beta(ab, alog, dtb, axis, nh):
    pre = ab + dtb
    softplus = jnp.maximum(pre, 0.0) + jnp.log(1.0 + jnp.exp(-jnp.abs(pre)))
    g = -jnp.exp(alog) * softplus * LOG2_E
    beta = jax.nn.sigmoid(ab)
    col = lax.broadcasted_iota(jnp.int32, ab.shape, 1 - axis)
    gf = _seg_cumsum(g, axis, False)
    gb = _seg_cumsum(g, axis, True)
    return jnp.where(col < nh, gf, jnp.where(col < N_DIR * nh, gb, beta))


def _inproj_kernel(x_ref, sc_ref, sh_ref, nw_ref, w_ref, wabt, alog, dtb, alogt, dtbt,
                   p_ref, gz_ref, qkv_ref, zs_ref, sga_ref, sgb_ref, gcs_ref, gcst_ref, *, nh, cw, offs):
    x = x_ref[...]
    h = x * lax.rsqrt(jnp.mean(x * x, axis=-1, keepdims=True) + NORM_EPS) * nw_ref[...]
    h = h * (1.0 + sc_ref[0]) + sh_ref[0]
    hb = h.astype(BF16)

    def proj(group, c0, width):
        start = offs[group] + c0
        return _dot(hb, w_ref[:, start:start + width])

    for c0 in range(0, p_ref.shape[1], cw):
        sl = slice(c0, c0 + cw)
        p_ref[:, sl] = (proj("bg", c0, cw) * proj("ax", c0, cw)).astype(BF16)
        gz_ref[:, sl] = (proj("cg", c0, cw) * _silu(proj("az", c0, cw))).astype(BF16)
    for c0 in range(0, qkv_ref.shape[1], cw):
        qkv_ref[:, c0:c0 + cw] = proj("qkv", c0, cw).astype(BF16)
    for c0 in range(0, zs_ref.shape[1], cw):
        zs_ref[:, c0:c0 + cw] = _silu(proj("z", c0, cw)).astype(BF16)
    for c0 in range(0, sga_ref.shape[1], cw):
        sga_ref[:, c0:c0 + cw] = jax.nn.sigmoid(proj("ga", c0, cw)).astype(BF16)
        sgb_ref[:, c0:c0 + cw] = jax.nn.sigmoid(proj("gb", c0, cw)).astype(BF16)

    gcs_ref[...] = _decay_and_beta(proj("ab", 0, gcs_ref.shape[1]), alog[...], dtb[...], 0, nh)
    gcst_ref[...] = _decay_and_beta(_dot_nt(wabt[...], hb), alogt[...], dtbt[...], 1, nh)


def _inproj(x2, scale, shift, norm_w, w_all, offs, widths, alog, dtb, *, seq, nh):
    rows, d = x2.shape
    tm = INPROJ_ROWS
    assert seq % tm == 0 and tm % CHUNK == 0
    tiles_per_seq = seq // tm
    nab = widths["ab"]
    wabt = w_all[:, offs["ab"]:offs["ab"] + nab].T
    resident = lambda a: pl.BlockSpec(a.shape, lambda i: (0,) * a.ndim, pipeline_mode=pl.Buffered(1))
    consts = (norm_w, w_all, wabt, alog, dtb, alog.T, dtb.T)
    row_blk = lambda w: pl.BlockSpec((tm, w), lambda i: (i, 0))
    mod_blk = pl.BlockSpec((1, 1, d), lambda i: (i // tiles_per_seq, 0, 0))
    out_shapes = [jax.ShapeDtypeStruct((rows, widths[g]), BF16) for g in ("bg", "cg", "qkv", "z", "ga", "gb")]
    out_shapes += [jax.ShapeDtypeStruct((rows, nab), F32), jax.ShapeDtypeStruct((nab, rows), F32)]
    out_specs = [row_blk(s.shape[1]) for s in out_shapes[:-1]]
    out_specs += [pl.BlockSpec((nab, tm), lambda i: (0, i))]
    return pl.pallas_call(
        functools.partial(_inproj_kernel, nh=nh, cw=INPROJ_COLS, offs=offs),
        grid=(rows // tm,),
        in_specs=[row_blk(d), mod_blk, mod_blk] + [resident(a) for a in consts],
        out_specs=out_specs,
        out_shape=out_shapes,
        compiler_params=pltpu.CompilerParams(dimension_semantics=("arbitrary",),
                                             vmem_limit_bytes=VMEM_LIMIT_BYTES),
        name="inproj",
    )(x2, scale, shift, *consts)


def _conv_rows(ext_ref, xm, prev, nxt, w_ref, width):
    r = xm.shape[0]
    half = width // 2
    pad = F32_SUBLANES
    ext_ref[pad:pad + r, :] = xm
    ext_ref[pad - half:pad, :] = prev[prev.shape[0] - half:, :]
    ext_ref[pad + r:pad + r + half, :] = nxt[:half, :]
    acc = None
    for t in range(width):
        term = ext_ref[pad - half + t:pad - half + t + r, :] * w_ref[t:t + 1, :]
        acc = term if acc is None else acc + term
    return acc


def _halo_specs(tile_rows, width_blk, col_of, row_tile_of, total_rows):
    per = tile_rows // BF16_SUBLANES
    last = total_rows // BF16_SUBLANES - 1

    def prev_map(*g):
        return (jnp.maximum(row_tile_of(*g) * per - 1, 0), col_of(*g))

    def next_map(*g):
        return (jnp.minimum((row_tile_of(*g) + 1) * per, last), col_of(*g))

    return (pl.BlockSpec((BF16_SUBLANES, width_blk), prev_map),
            pl.BlockSpec((BF16_SUBLANES, width_blk), next_map))


GROUP = 128


def _half_rows(a, b, second):
    start = b if second else 0
    return jnp.concatenate([a[k + start:k + start + b] for k in range(0, a.shape[0], 2 * b)], axis=0)


def _spread_rows(c, b, second, other):
    parts = []
    for k in range(0, c.shape[0], b):
        pair = (other[k:k + b], c[k:k + b])
        parts.extend(pair if second else pair[::-1])
    return jnp.concatenate(parts, axis=0)


def _zero_row_after(v):
    bits = pltpu.bitcast(v[:F32_SUBLANES], jnp.int32)
    zero = lax.shift_right_logical(lax.shift_right_logical(bits, 16), 16)
    return zero[:1].astype(F32)


def _block_inverses(pmats, lowers, ii, jj, fillers=()):
    same_block = lambda s: (ii >> s) == (jj >> s)
    eye = jnp.where(ii == jj, 1.0, 0.0)
    xs = [eye + jnp.where(same_block(1), p, 0.0) for p in pmats]
    s = 1
    while (1 << s) < CHUNK:
        b = 1 << s
        off = same_block(s + 1) & jnp.logical_not(same_block(s))
        offs = [jnp.where(off, p, 0.0) for p in pmats]
        xbs = [x.astype(BF16) for x in xs]
        if b < F32_SUBLANES:
            ys = [_dot(o.astype(BF16), xb) for o, xb in zip(offs, xbs)]
            xs = [x + _dot(xb, y.astype(BF16)) for x, xb, y in zip(xs, xbs, ys)]
        else:
            zeros = jnp.zeros((GROUP // 2, GROUP), F32)
            ys = [_dot(_half_rows(o, b, lo).astype(BF16), xb) for o, xb, lo in zip(offs, xbs, lowers)]
            ups = [_dot(_half_rows(x, b, lo).astype(BF16), _spread_rows(y, b, lo, zeros).astype(BF16))
                   for x, y, lo in zip(xs, ys, lowers)]
            xs = [_spread_rows(_half_rows(x, b, lo) + up, b, lo, _half_rows(x, b, not lo))
                  for x, up, lo in zip(xs, ups, lowers)]
        if s - 1 < len(fillers) and fillers[s - 1] is not None:
            fillers[s - 1](_zero_row_after(ys[-1]))
        s += 1
    assert len(fillers) < s
    return xs


def _prep_kernel(qm, qp, qn, km, kp, kn, vm, vp, vn, cwq, cwk, cwv, gcs_ref, gcst_ref,
                 lhs1_ref, lhs2_ref, u_ref, gl_ref, ext_ref, qkv_s, col_s, row_s, *, nrb, nh):
    i = pl.program_id(0)
    j = jnp.minimum(i, pl.num_programs(0) - 2)
    r = (j // nh) % nrb
    h = j % nh
    first = r == 0
    last = r == nrb - 1

    @pl.when(i == 0)
    def _():
        qkv_s[...] = jnp.zeros_like(qkv_s)
        col_s[...] = jnp.zeros_like(col_s)
        row_s[...] = jnp.zeros_like(row_s)

    q, k, v = qkv_s[0], qkv_s[1], qkv_s[2]
    g_col = (col_s[0], col_s[1])
    b_col = (col_s[2], col_s[3])
    nbc = (-b_col[0], -b_col[1])
    g_row = (row_s[0], row_s[1])

    def conv_silu(idx, m, p, n, cw, zero_row):
        pv = jnp.where(first, 0.0, p[...].astype(F32))
        nx = jnp.where(last, 0.0, n[...].astype(F32))
        w = cw[...] + zero_row
        return _silu(_conv_rows(ext_ref.at[idx], m[...].astype(F32), pv, nx, w, CONV_QKV_WIDTH))

    def stage1_q(zero_row):
        q1 = conv_silu(0, qm, qp, qn, cwq, zero_row)
        qkv_s[0] = q1 * lax.rsqrt(jnp.sum(q1 * q1, axis=-1, keepdims=True) + L2_EPS) * (HEAD_DIM ** -0.5)

    def stage1_k(zero_row):
        k1 = conv_silu(1, km, kp, kn, cwk, zero_row)
        qkv_s[1] = k1 * lax.rsqrt(jnp.sum(k1 * k1, axis=-1, keepdims=True) + L2_EPS)

    def stage1_v(zero_row):
        qkv_s[2] = conv_silu(2, vm, vp, vn, cwv, zero_row)

    def stage1_gates(zero_row):
        gcs = gcs_ref[...] + zero_row[:, :gcs_ref.shape[1]]
        lane = lax.broadcasted_iota(jnp.int32, gcs.shape, 1)
        for slot in range(4):
            picked = jnp.sum(jnp.where(lane == slot * nh + h, gcs, 0.0), axis=1, keepdims=True)
            col_s[slot] = jnp.broadcast_to(picked, col_s.shape[1:])
        for d in range(N_DIR):
            row_s[d] = gcst_ref[pl.ds(d * nh + h, 1), :]

    ii = lax.broadcasted_iota(jnp.int32, (GROUP, GROUP), 0)
    jj = lax.broadcasted_iota(jnp.int32, (GROUP, GROUP), 1)
    same_chunk = (ii // CHUNK) == (jj // CHUNK)
    cpg = GROUP // CHUNK

    inst = []
    for gi in range(q.shape[0] // GROUP):
        rs = slice(gi * GROUP, (gi + 1) * GROUP)
        kc, qc, vc = k[rs], q[rs], v[rs]
        kq = _dot_nt(jnp.concatenate([kc, qc], axis=0).astype(BF16), kc.astype(BF16))
        kk, qk = kq[:GROUP], kq[GROUP:]
        for d in range(N_DIR):
            gc = g_col[d][rs]
            gr = g_row[d][:, rs]
            bc = b_col[d][rs]
            incl = same_chunk & ((ii >= jj) if d == 0 else (ii <= jj))
            strict = same_chunk & ((ii > jj) if d == 0 else (ii < jj))
            decay = jnp.exp2(gc - gr)
            pmat = jnp.where(strict, (nbc[d][rs] * kk) * decay, 0.0)
            aqk = jnp.where(incl, qk * decay, 0.0)
            inst.append(dict(gi=gi, d=d, kc=kc, qc=qc, vc=vc, gc=gc, bc=bc, pmat=pmat, aqk=aqk))

    tmats = _block_inverses([it["pmat"] for it in inst], [it["d"] == 0 for it in inst], ii, jj,
                            fillers=[None, stage1_gates, stage1_q, stage1_k, stage1_v])

    for it, tmat in zip(inst, tmats):
        gi, d, kc, qc, vc, gc, bc, aqk = (it[n] for n in ("gi", "d", "kc", "qc", "vc", "gc", "bc", "aqk"))
        eg = jnp.exp2(gc)
        rhs = jnp.concatenate([vc * bc, kc * (bc * eg)], axis=1).astype(BF16)
        uw = _dot(tmat.astype(BF16), rhs)
        edge = CHUNK - 1 if d == 0 else 0
        g_tot = jnp.concatenate(
            [jnp.broadcast_to(gc[c * CHUNK + edge:c * CHUNK + edge + 1], (CHUNK, HEAD_DIM)) for c in range(cpg)],
            axis=0)
        kdt = (kc * jnp.exp2(g_tot - gc)).T
        qd = qc * eg
        for c in range(cpg):
            rs = slice(c * CHUNK, (c + 1) * CHUNK)
            cc = gi * cpg + c
            lhs1_ref[d, cc] = jnp.concatenate([uw[rs, HEAD_DIM:], qd[rs]], axis=0).astype(BF16)
            u_ref[d, cc] = uw[rs, :HEAD_DIM].astype(BF16)
            gl_ref[d, cc] = jnp.exp2(g_tot[c * CHUNK:c * CHUNK + 1])
        for p in range(cpg // 2):
            ls = slice(p * 2 * CHUNK, (p + 1) * 2 * CHUNK)
            a_pair = aqk[p * 2 * CHUNK:p * 2 * CHUNK + CHUNK, ls] + aqk[p * 2 * CHUNK + CHUNK:(p + 1) * 2 * CHUNK, ls]
            lhs2_ref[d, gi * (cpg // 2) + p] = jnp.concatenate([a_pair, kdt[:, ls]], axis=0).astype(BF16)


def _prep(qkv, conv_w, gcs, gcst, *, bsz, seq, nh):
    rows = qkv.shape[0]
    rr = PREP_ROWS
    nrb = seq // rr
    ncb = rr // CHUNK
    nc = seq // CHUNK
    ngc = gcs.shape[1]
    n_tiles = bsz * nrb * nh
    assert GROUP == HEAD_DIM and rr % GROUP == 0

    tile_in = lambda i: jnp.minimum(i, n_tiles - 1)
    tile_out = lambda i: jnp.maximum(i - 1, 0)
    row_of = lambda i: tile_in(i) // nh
    head_of = lambda i: tile_in(i) % nh
    in_specs = []
    for part in range(3):
        col_of = functools.partial(lambda i, part: part * nh + head_of(i), part=part)
        in_specs.append(pl.BlockSpec((rr, HEAD_DIM), functools.partial(
            lambda i, col_of: (row_of(i), col_of(i)), col_of=col_of)))
        in_specs.extend(_halo_specs(rr, HEAD_DIM, col_of, row_of, rows))
    for part in range(3):
        in_specs.append(pl.BlockSpec((CONV_QKV_WIDTH, HEAD_DIM), functools.partial(
            lambda i, part: (0, part * nh + head_of(i)), part=part)))
    in_specs.append(pl.BlockSpec((rr, ngc), lambda i: (row_of(i), 0)))
    in_specs.append(pl.BlockSpec((ngc, rr), lambda i: (0, row_of(i))))

    def out(per_block, shape_tail, dtype):
        blk = (None, N_DIR, None, per_block) + shape_tail
        zeros = (0,) * len(shape_tail)

        def imap(i):
            t = tile_out(i)
            return (t // (nrb * nh), 0, t % nh, (t // nh) % nrb) + zeros

        return (jax.ShapeDtypeStruct((bsz, N_DIR, nh, per_block * nrb) + shape_tail, dtype),
                pl.BlockSpec(blk, imap))

    outs = [out(ncb, (2 * CHUNK, HEAD_DIM), BF16), out(ncb // 2, (CHUNK + HEAD_DIM, 2 * CHUNK), BF16),
            out(ncb, (CHUNK, HEAD_DIM), BF16), out(ncb, (1, HEAD_DIM), F32)]
    return pl.pallas_call(
        functools.partial(_prep_kernel, nrb=nrb, nh=nh),
        grid=(n_tiles + 1,),
        in_specs=in_specs,
        out_specs=[o[1] for o in outs],
        out_shape=[o[0] for o in outs],
        scratch_shapes=[pltpu.VMEM((3, rr + 2 * F32_SUBLANES, HEAD_DIM), F32),
                        pltpu.VMEM((3, rr, HEAD_DIM), F32),
                        pltpu.VMEM((4, rr, HEAD_DIM), F32),
                        pltpu.VMEM((N_DIR, 1, rr), F32)],
        compiler_params=pltpu.CompilerParams(dimension_semantics=("arbitrary",),
                                             vmem_limit_bytes=VMEM_LIMIT_BYTES),
        name="prep",
    )(qkv, qkv, qkv, qkv, qkv, qkv, qkv, qkv, qkv, conv_w, conv_w, conv_w, gcs, gcst)


def _scan_kernel(l1f, l2f, uf, glf, l1b, l2b, ub, glb, of_ref, ob_ref, s_ref, *, ncb, nh):
    @pl.when(pl.program_id(1) == 0)
    def _():
        s_ref[...] = jnp.zeros_like(s_ref)

    dirs = ((l1f, l2f, uf, glf, of_ref), (l1b, l2b, ub, glb, ob_ref))

    zero_rows = jnp.zeros((CHUNK, HEAD_DIM), BF16)

    def chunk_step(pair, parity):
        chains = [(d, h, pair[d], 2 * pair[d] + parity[d]) for d in range(N_DIR) for h in range(nh)]
        states = [s_ref[d, h] for d, h, _, _ in chains]
        r1s = [_dot(dirs[d][0][h, cc], s.astype(BF16))
               for (d, h, _, cc), s in zip(chains, states)]
        v_news = [dirs[d][2][h, cc].astype(F32) - r1[:CHUNK] for (d, h, _, cc), r1 in zip(chains, r1s)]
        r2s = []
        for (d, h, pc, _), v in zip(chains, v_news):
            vb = v.astype(BF16)
            rhs = jnp.concatenate([zero_rows, vb] if parity[d] else [vb, zero_rows], axis=0)
            r2s.append(_dot(dirs[d][1][h, pc], rhs))
        for (d, h, _, cc), s, r1, r2 in zip(chains, states, r1s, r2s):
            s_ref[d, h] = s * dirs[d][3][h, cc] + r2[CHUNK:]
            r0 = pl.multiple_of(cc * CHUNK, CHUNK)
            dirs[d][4][pl.ds(r0, CHUNK), h * HEAD_DIM:(h + 1) * HEAD_DIM] = (
                r1[CHUNK:] + r2[:CHUNK]).astype(dirs[d][4].dtype)

    def body(pi, carry):
        pair = (pi, ncb // 2 - 1 - pi)
        chunk_step(pair, (0, 1))
        chunk_step(pair, (1, 0))
        return carry

    lax.fori_loop(0, ncb // 2, body, 0)


def _scan(lhs1, lhs2, u, gl, *, seq):
    bsz, _, nh, nc = lhs1.shape[:4]
    rr = SCAN_ROWS
    ncb = rr // CHUNK
    nt = nc // ncb

    def specs(d):
        def mk(arr):
            tail = arr.shape[4:]
            zeros = (0,) * len(tail)
            if d == 0:
                imap = lambda b, t: (b, 0, 0, t) + zeros
            else:
                imap = lambda b, t: (b, 1, 0, nt - 1 - t) + zeros
            return pl.BlockSpec((None, None, nh, arr.shape[3] // nt) + tail, imap)
        return [mk(a) for a in (lhs1, lhs2, u, gl)]

    o_shape = jax.ShapeDtypeStruct((bsz * seq, nh * HEAD_DIM), BF16)
    return pl.pallas_call(
        functools.partial(_scan_kernel, ncb=ncb, nh=nh),
        grid=(bsz, nt),
        in_specs=specs(0) + specs(1),
        out_specs=[pl.BlockSpec((rr, nh * HEAD_DIM), lambda b, t: (b * nt + t, 0)),
                   pl.BlockSpec((rr, nh * HEAD_DIM), lambda b, t: (b * nt + nt - 1 - t, 0))],
        out_shape=[o_shape, o_shape],
        scratch_shapes=[pltpu.VMEM((N_DIR, nh, HEAD_DIM, HEAD_DIM), F32)],
        compiler_params=pltpu.CompilerParams(dimension_semantics=("arbitrary", "arbitrary"),
                                             vmem_limit_bytes=VMEM_LIMIT_BYTES),
        name="scan",
    )(lhs1, lhs2, u, gl, lhs1, lhs2, u, gl)


def _out_kernel(x_ref, of_ref, ob_ref, zs_ref, pm, pp, pn, gz_ref, sga_ref, sgb_ref, gate_ref,
                cwa_ref, gnw_ref, fnw_ref, wpa, wpb, wo, y_ref, ext_ref, *, tiles_per_seq, nh, final):
    r = pl.program_id(0) % tiles_per_seq
    sub = x_ref.shape[0] // OUT_SUBTILES
    rows = [slice(i * sub, (i + 1) * sub) for i in range(OUT_SUBTILES)]

    pv = jnp.where(r == 0, 0.0, pp[...].astype(F32))
    nx = jnp.where(r == tiles_per_seq - 1, 0.0, pn[...].astype(F32))
    conv = _conv_rows(ext_ref, pm[...].astype(F32), pv, nx, cwa_ref, CONV_A_WIDTH)

    def yb_input(rs):
        o = of_ref[rs, :].astype(F32) + ob_ref[rs, :].astype(F32)
        heads = []
        for h in range(nh):
            oh = o[:, h * HEAD_DIM:(h + 1) * HEAD_DIM]
            heads.append(oh * lax.rsqrt(jnp.mean(oh * oh, axis=-1, keepdims=True) + NORM_EPS) * gnw_ref[...])
        return (jnp.concatenate(heads, axis=1) * zs_ref[rs, :].astype(F32)).astype(BF16)

    yb_ins = [yb_input(rs) for rs in rows]
    ya_ins = [(gz_ref[rs, :].astype(F32) * conv[rs]).astype(BF16) for rs in rows]
    ybs = [_dot(v, wpb[...]) for v in yb_ins]
    yas = [_dot(v, wpa[...]) for v in ya_ins]
    mergeds = [(sga_ref[rs, :].astype(F32) * ya + sgb_ref[rs, :].astype(F32) * yb).astype(BF16)
               for rs, ya, yb in zip(rows, yas, ybs)]
    deltas = [_dot(m, wo[...]) for m in mergeds]
    for rs, delta in zip(rows, deltas):
        xn = x_ref[rs, :] + gate_ref[0] * delta
        if final:
            xn = xn * lax.rsqrt(jnp.mean(xn * xn, axis=-1, keepdims=True) + NORM_EPS) * fnw_ref[...]
        y_ref[rs, :] = xn


def _out(x2, o_f, o_b, zs, p, gz, sga, sgb, gate, conv_a_w, gdn_norm_w, final_norm_w, wpa, wpb, wo,
         *, seq, nh, final):
    rows, d = x2.shape
    tm = OUT_ROWS
    tiles_per_seq = seq // tm
    row_blk = lambda w: pl.BlockSpec((tm, w), lambda i: (i, 0))
    const = lambda a: pl.BlockSpec(a.shape, lambda i: (0,) * a.ndim)
    halo_prev, halo_next = _halo_specs(tm, p.shape[1], lambda i: 0, lambda i: i, rows)
    return pl.pallas_call(
        functools.partial(_out_kernel, tiles_per_seq=tiles_per_seq, nh=nh, final=final),
        grid=(rows // tm,),
        in_specs=[row_blk(d), row_blk(o_f.shape[1]), row_blk(o_b.shape[1]), row_blk(zs.shape[1]),
                  row_blk(p.shape[1]), halo_prev, halo_next, row_blk(gz.shape[1]),
                  row_blk(sga.shape[1]), row_blk(sgb.shape[1]),
                  pl.BlockSpec((1, 1, d), lambda i: (i // tiles_per_seq, 0, 0)),
                  const(conv_a_w), const(gdn_norm_w), const(final_norm_w),
                  const(wpa), const(wpb), const(wo)],
        out_specs=row_blk(d),
        out_shape=jax.ShapeDtypeStruct((rows, d), F32),
        scratch_shapes=[pltpu.VMEM((tm + 2 * F32_SUBLANES, p.shape[1]), F32)],
        compiler_params=pltpu.CompilerParams(dimension_semantics=("arbitrary",),
                                             vmem_limit_bytes=VMEM_LIMIT_BYTES),
        name="out",
    )(x2, o_f, o_b, zs, p, p, p, gz, sga, sgb, gate, conv_a_w, gdn_norm_w, final_norm_w, wpa, wpb, wo)


def kernel(x, c, w_ada, b_ada, norm_w, w_in, conv_a_w, conv_qkv_w, a_log, dt_bias, gdn_norm_w,
           w_pa, w_pb, w_o, final_norm_w):
    bsz, seq, d = x.shape
    depth = w_ada.shape[0]
    nh = a_log.shape[2]
    d_conv = conv_a_w.shape[2]
    key_dim = nh * HEAD_DIM
    assert conv_qkv_w.shape[2] == 3 * key_dim and gdn_norm_w.shape[1] == HEAD_DIM
    assert seq % PREP_ROWS == 0 and seq % SCAN_ROWS == 0 and seq % OUT_ROWS == 0
    n_gate = N_DIR * nh
    widths = dict(bg=d_conv, cg=d_conv, ax=d_conv, az=d_conv, qkv=3 * key_dim, z=key_dim,
                  ab=2 * n_gate, ga=d, gb=d)
    assert sum(widths.values()) == w_in.shape[2]
    src, start = {}, 0
    for g, wd in widths.items():
        src[g] = start
        start += wd
    order = [g for g in widths if g != "ab"] + ["ab"]
    offs, start = {}, 0
    for g in order:
        offs[g] = start
        start += widths[g]
    assert all(o % HEAD_DIM == 0 for o in offs.values())

    x2 = x.reshape(bsz * seq, d)
    c_pad = jnp.zeros((8, d), F32).at[:bsz].set(c)
    for l in range(depth):
        mod = _ada(c_pad, w_ada[l], b_ada[l][None, :])[:bsz]
        shift, scale, gate = (mod[:, i * d:(i + 1) * d][:, None, :] for i in range(3))

        w_all = jnp.concatenate([w_in[l][:, src[g]:src[g] + widths[g]] for g in order], axis=1).astype(BF16)
        zeros = jnp.zeros((1, n_gate), F32)
        alog = jnp.concatenate([a_log[l].reshape(1, n_gate), zeros], axis=1)
        dtb = jnp.concatenate([dt_bias[l].reshape(1, n_gate), zeros], axis=1)
        p, gz, qkv, zs, sga, sgb, gcs, gcst = _inproj(
            x2, scale, shift, norm_w[l][None, :], w_all, offs, widths, alog, dtb, seq=seq, nh=nh)

        lhs1, lhs2, u, gl = _prep(qkv, conv_qkv_w[l], gcs, gcst, bsz=bsz, seq=seq, nh=nh)
        o_f, o_b = _scan(lhs1, lhs2, u, gl, seq=seq)

        x2 = _out(x2, o_f, o_b, zs, p, gz, sga, sgb, gate, conv_a_w[l], gdn_norm_w[l][None, :],
                  final_norm_w[None, :], w_pa[l].astype(BF16), w_pb[l].astype(BF16), w_o[l].astype(BF16),
                  seq=seq, nh=nh, final=(l == depth - 1))
    return x2.reshape(bsz, seq, d)
```

```python
import functools

import jax
import jax.numpy as jnp
from jax import lax
from jax.experimental import pallas as pl
from jax.experimental.pallas import tpu as pltpu

F32 = jnp.float32
BF16 = jnp.bfloat16

HEAD_DIM = 128
CHUNK = 64
N_DIR = 2
CONV_A_WIDTH = 3
CONV_QKV_WIDTH = 5
NORM_EPS = 1e-6
L2_EPS = 1e-6
LOG2_E = 1.4426950408889634
BF16_SUBLANES = 16
F32_SUBLANES = 8
VMEM_LIMIT_BYTES = 56 * 1024 * 1024

INPROJ_ROWS = 512
INPROJ_COLS = 512
PREP_ROWS = 1024
SCAN_ROWS = 512
OUT_ROWS = 512
OUT_SUBTILES = 2


def _dot(a, b):
    return jnp.dot(a, b, preferred_element_type=F32)


def _dot_nt(a, b):
    return lax.dot_general(a, b, (((1,), (1,)), ((), ())), preferred_element_type=F32)


def _silu(x):
    return x * jax.nn.sigmoid(x)


def _ada_kernel(c_ref, w_ref, b_ref, o_ref):
    sc = _silu(c_ref[...])
    o_ref[...] = jnp.dot(sc, w_ref[...], preferred_element_type=F32,
                         precision=lax.Precision.HIGHEST) + b_ref[...]


def _ada(c_pad, w, b):
    m, d = c_pad.shape
    n = w.shape[1]
    bn = d
    return pl.pallas_call(
        _ada_kernel,
        grid=(n // bn,),
        in_specs=[pl.BlockSpec((m, d), lambda j: (0, 0)),
                  pl.BlockSpec((d, bn), lambda j: (0, j)),
                  pl.BlockSpec((1, bn), lambda j: (0, j))],
        out_specs=pl.BlockSpec((m, bn), lambda j: (0, j)),
        out_shape=jax.ShapeDtypeStruct((m, n), F32),
        name="ada",
    )(c_pad, w, b)


def _seg_cumsum(x, axis, reverse):
    n = x.shape[axis]
    pos = lax.broadcasted_iota(jnp.int32, x.shape, axis) % CHUNK
    s = 1
    while s < CHUNK:
        if reverse:
            x = x + jnp.where(pos < CHUNK - s, pltpu.roll(x, n - s, axis), 0.0)
        else:
            x = x + jnp.where(pos >= s, pltpu.roll(x, s, axis), 0.0)
        s *= 2
    return x


def _decay_and_beta(ab, alog, dtb, axis, nh):
    pre = ab + dtb
    softplus = jnp.maximum(pre, 0.0) + jnp.log(1.0 + jnp.exp(-jnp.abs(pre)))
    g = -jnp.exp(alog) * softplus * LOG2_E
    beta = jax.nn.sigmoid(ab)
    col = lax.broadcasted_iota(jnp.int32, ab.shape, 1 - axis)
    gf = _seg_cumsum(g, axis, False)
    gb = _seg_cumsum(g, axis, True)
    return jnp.where(col < nh, gf, jnp.where(col < N_DIR * nh, gb, beta))


def _inproj_kernel(x_ref, sc_ref, sh_ref, nw_ref, w_ref, wg_ref, wabt, alog, dtb, alogt, dtbt,
                   p_ref, gz_ref, qkv_ref, zs_ref, sga_ref, sgb_ref, gcs_ref, gcst_ref, *, nh, cw, offs):
    x = x_ref[...]
    h = x * lax.rsqrt(jnp.mean(x * x, axis=-1, keepdims=True) + NORM_EPS) * nw_ref[...]
    h = h * (1.0 + sc_ref[0]) + sh_ref[0]
    hb = h.astype(BF16)

    def proj(group, c0, width):
        which, start = offs[group]
        ref = (w_ref, wg_ref)[which]
        return _dot(hb, ref[:, start + c0:start + c0 + width])

    for c0 in range(0, p_ref.shape[1], cw):
        sl = slice(c0, c0 + cw)
        p_ref[:, sl] = (proj("bg", c0, cw) * proj("ax", c0, cw)).astype(BF16)
        gz_ref[:, sl] = (proj("cg", c0, cw) * _silu(proj("az", c0, cw))).astype(BF16)
    for c0 in range(0, qkv_ref.shape[1], cw):
        qkv_ref[:, c0:c0 + cw] = proj("qkv", c0, cw).astype(BF16)
    for c0 in range(0, zs_ref.shape[1], cw):
        zs_ref[:, c0:c0 + cw] = _silu(proj("z", c0, cw)).astype(BF16)
    for c0 in range(0, sga_ref.shape[1], cw):
        sga_ref[:, c0:c0 + cw] = jax.nn.sigmoid(proj("ga", c0, cw)).astype(BF16)
        sgb_ref[:, c0:c0 + cw] = jax.nn.sigmoid(proj("gb", c0, cw)).astype(BF16)

    gcs_ref[...] = _decay_and_beta(proj("ab", 0, gcs_ref.shape[1]), alog[...], dtb[...], 0, nh)
    gcst_ref[...] = _decay_and_beta(_dot_nt(wabt[...], hb), alogt[...], dtbt[...], 1, nh)


def _inproj(x2, scale, shift, norm_w, w_main, w_gates, offs, widths, alog, dtb, *, seq, nh):
    rows, d = x2.shape
    tm = INPROJ_ROWS
    assert seq % tm == 0 and tm % CHUNK == 0
    tiles_per_seq = seq // tm
    nab = widths["ab"]
    wabt = w_main[:, offs["ab"][1]:offs["ab"][1] + nab].T
    resident = lambda a: pl.BlockSpec(a.shape, lambda i: (0,) * a.ndim, pipeline_mode=pl.Buffered(1))
    consts = (norm_w, w_main, w_gates, wabt, alog, dtb, alog.T, dtb.T)
    row_blk = lambda w: pl.BlockSpec((tm, w), lambda i: (i, 0))
    mod_blk = pl.BlockSpec((1, 1, d), lambda i: (i // tiles_per_seq, 0, 0))
    out_shapes = [jax.ShapeDtypeStruct((rows, widths[g]), BF16) for g in ("bg", "cg", "qkv", "z", "ga", "gb")]
    out_shapes += [jax.ShapeDtypeStruct((rows, nab), F32), jax.ShapeDtypeStruct((nab, rows), F32)]
    out_specs = [row_blk(s.shape[1]) for s in out_shapes[:-1]]
    out_specs += [pl.BlockSpec((nab, tm), lambda i: (0, i))]
    return pl.pallas_call(
        functools.partial(_inproj_kernel, nh=nh, cw=INPROJ_COLS, offs=offs),
        grid=(rows // tm,),
        in_specs=[row_blk(d), mod_blk, mod_blk] + [resident(a) for a in consts],
        out_specs=out_specs,
        out_shape=out_shapes,
        compiler_params=pltpu.CompilerParams(dimension_semantics=("arbitrary",),
                                             vmem_limit_bytes=VMEM_LIMIT_BYTES),
        name="inproj",
    )(x2, scale, shift, *consts)


def _conv_rows(ext_ref, xm, prev, nxt, w_ref, width):
    r = xm.shape[0]
    half = width // 2
    pad = F32_SUBLANES
    ext_ref[pad:pad + r, :] = xm
    ext_ref[pad - half:pad, :] = prev[prev.shape[0] - half:, :]
    ext_ref[pad + r:pad + r + half, :] = nxt[:half, :]
    acc = None
    for t in range(width):
        term = ext_ref[pad - half + t:pad - half + t + r, :] * w_ref[t:t + 1, :]
        acc = term if acc is None else acc + term
    return acc


def _halo_specs(tile_rows, width_blk, col_of, row_tile_of, total_rows):
    per = tile_rows // BF16_SUBLANES
    last = total_rows // BF16_SUBLANES - 1

    def prev_map(*g):
        return (jnp.maximum(row_tile_of(*g) * per - 1, 0), col_of(*g))

    def next_map(*g):
        return (jnp.minimum((row_tile_of(*g) + 1) * per, last), col_of(*g))

    return (pl.BlockSpec((BF16_SUBLANES, width_blk), prev_map),
            pl.BlockSpec((BF16_SUBLANES, width_blk), next_map))


GROUP = 128


def _half_rows(a, b, second):
    start = b if second else 0
    return jnp.concatenate([a[k + start:k + start + b] for k in range(0, a.shape[0], 2 * b)], axis=0)


def _spread_rows(c, b, second, other):
    parts = []
    for k in range(0, c.shape[0], b):
        pair = (other[k:k + b], c[k:k + b])
        parts.extend(pair if second else pair[::-1])
    return jnp.concatenate(parts, axis=0)


def _zero_row_after(v):
    bits = pltpu.bitcast(v[:F32_SUBLANES], jnp.int32)
    zero = lax.shift_right_logical(lax.shift_right_logical(bits, 16), 16)
    return zero[:1].astype(F32)


def _block_inverses(pmats, lowers, ii, jj, fillers=()):
    same_block = lambda s: (ii >> s) == (jj >> s)
    eye = jnp.where(ii == jj, 1.0, 0.0)
    xs = [eye + jnp.where(same_block(1), p, 0.0) for p in pmats]
    s = 1
    while (1 << s) < CHUNK:
        b = 1 << s
        off = same_block(s + 1) & jnp.logical_not(same_block(s))
        offs = [jnp.where(off, p, 0.0) for p in pmats]
        xbs = [x.astype(BF16) for x in xs]
        if b < F32_SUBLANES:
            ys = [_dot(o.astype(BF16), xb) for o, xb in zip(offs, xbs)]
            xs = [x + _dot(xb, y.astype(BF16)) for x, xb, y in zip(xs, xbs, ys)]
        else:
            zeros = jnp.zeros((GROUP // 2, GROUP), F32)
            ys = [_dot(_half_rows(o, b, lo).astype(BF16), xb) for o, xb, lo in zip(offs, xbs, lowers)]
            ups = [_dot(_half_rows(x, b, lo).astype(BF16), _spread_rows(y, b, lo, zeros).astype(BF16))
                   for x, y, lo in zip(xs, ys, lowers)]
            xs = [_spread_rows(_half_rows(x, b, lo) + up, b, lo, _half_rows(x, b, not lo))
                  for x, up, lo in zip(xs, ups, lowers)]
        if s - 1 < len(fillers) and fillers[s - 1] is not None:
            fillers[s - 1](_zero_row_after(ys[-1]))
        s += 1
    assert len(fillers) < s
    return xs


def _prep_kernel(qm, qp, qn, km, kp, kn, vm, vp, vn, cwq, cwk, cwv, gcs_ref, gcst_ref,
                 lhs1_ref, lhs2_ref, u_ref, gl_ref, ext_ref, qkv_s, col_s, row_s, *, nrb, nh):
    i = pl.program_id(0)
    j = jnp.minimum(i, pl.num_programs(0) - 2)
    r = (j // nh) % nrb
    h = j % nh
    first = r == 0
    last = r == nrb - 1

    @pl.when(i == 0)
    def _():
        qkv_s[...] = jnp.zeros_like(qkv_s)
        col_s[...] = jnp.zeros_like(col_s)
        row_s[...] = jnp.zeros_like(row_s)

    q, k, v = qkv_s[0], qkv_s[1], qkv_s[2]
    g_col = (col_s[0], col_s[1])
    b_col = (col_s[2], col_s[3])
    nbc = (-b_col[0], -b_col[1])
    g_row = (row_s[0], row_s[1])

    def conv_silu(idx, m, p, n, cw, zero_row):
        pv = jnp.where(first, 0.0, p[...].astype(F32))
        nx = jnp.where(last, 0.0, n[...].astype(F32))
        w = cw[...] + zero_row
        return _silu(_conv_rows(ext_ref.at[idx], m[...].astype(F32), pv, nx, w, CONV_QKV_WIDTH))

    def stage1_q(zero_row):
        q1 = conv_silu(0, qm, qp, qn, cwq, zero_row)
        qkv_s[0] = q1 * (lax.rsqrt(jnp.sum(q1 * q1, axis=-1, keepdims=True) + L2_EPS) * (HEAD_DIM ** -0.5))

    def stage1_k(zero_row):
        k1 = conv_silu(1, km, kp, kn, cwk, zero_row)
        qkv_s[1] = k1 * lax.rsqrt(jnp.sum(k1 * k1, axis=-1, keepdims=True) + L2_EPS)

    def stage1_v(zero_row):
        qkv_s[2] = conv_silu(2, vm, vp, vn, cwv, zero_row)

    def stage1_gates(zero_row):
        del zero_row
        gcs = gcs_ref[...]
        lane = lax.broadcasted_iota(jnp.int32, gcs.shape, 1)
        for slot in range(4):
            picked = jnp.sum(jnp.where(lane == slot * nh + h, gcs, 0.0), axis=1, keepdims=True)
            col_s[slot] = jnp.broadcast_to(picked, col_s.shape[1:])
        for d in range(N_DIR):
            row_s[d] = gcst_ref[pl.ds(d * nh + h, 1), :]

    ii = lax.broadcasted_iota(jnp.int32, (GROUP, GROUP), 0)
    jj = lax.broadcasted_iota(jnp.int32, (GROUP, GROUP), 1)
    same_chunk = (ii // CHUNK) == (jj // CHUNK)
    cpg = GROUP // CHUNK

    inst = []
    for gi in range(q.shape[0] // GROUP):
        rs = slice(gi * GROUP, (gi + 1) * GROUP)
        kc, qc, vc = k[rs], q[rs], v[rs]
        kq = _dot_nt(jnp.concatenate([kc, qc], axis=0).astype(BF16), kc.astype(BF16))
        kk, qk = kq[:GROUP], kq[GROUP:]
        for d in range(N_DIR):
            gc = g_col[d][rs]
            gr = g_row[d][:, rs]
            bc = b_col[d][rs]
            incl = same_chunk & ((ii >= jj) if d == 0 else (ii <= jj))
            strict = same_chunk & ((ii > jj) if d == 0 else (ii < jj))
            decay = jnp.exp2(gc - gr)
            pmat = jnp.where(strict, (nbc[d][rs] * kk) * decay, 0.0)
            aqk = jnp.where(incl, qk * decay, 0.0)
            eg = jnp.exp2(gc)
            edge = CHUNK - 1 if d == 0 else 0
            g_tot = jnp.concatenate(
                [jnp.broadcast_to(gc[c * CHUNK + edge:c * CHUNK + edge + 1], (CHUNK, HEAD_DIM))
                 for c in range(cpg)], axis=0)
            kdt = (kc * jnp.exp2(g_tot - gc)).T
            qd = qc * eg
            for c in range(cpg):
                cs = slice(c * CHUNK, (c + 1) * CHUNK)
                lhs1_ref[d, gi * cpg + c, CHUNK:, :] = qd[cs].astype(BF16)
                gl_ref[d, gi * cpg + c] = jnp.exp2(g_tot[c * CHUNK:c * CHUNK + 1])
            for p in range(cpg // 2):
                ls = slice(p * 2 * CHUNK, (p + 1) * 2 * CHUNK)
                a_pair = (aqk[p * 2 * CHUNK:p * 2 * CHUNK + CHUNK, ls]
                          + aqk[p * 2 * CHUNK + CHUNK:(p + 1) * 2 * CHUNK, ls])
                lhs2_ref[d, gi * (cpg // 2) + p] = jnp.concatenate([a_pair, kdt[:, ls]], axis=0).astype(BF16)
            rhs = jnp.concatenate([vc * bc, kc * (bc * eg)], axis=1).astype(BF16)
            inst.append(dict(gi=gi, d=d, pmat=pmat, rhs=rhs))

    tmats = _block_inverses([it["pmat"] for it in inst], [it["d"] == 0 for it in inst], ii, jj,
                            fillers=[stage1_gates, stage1_q, stage1_k, stage1_v])

    for it, tmat in zip(inst, tmats):
        gi, d = it["gi"], it["d"]
        uw = _dot(tmat.astype(BF16), it["rhs"])
        for c in range(cpg):
            cs = slice(c * CHUNK, (c + 1) * CHUNK)
            lhs1_ref[d, gi * cpg + c, :CHUNK, :] = uw[cs, HEAD_DIM:].astype(BF16)
            u_ref[d, gi * cpg + c] = uw[cs, :HEAD_DIM].astype(BF16)


def _prep(qkv, conv_w, gcs, gcst, *, bsz, seq, nh):
    rows = qkv.shape[0]
    rr = PREP_ROWS
    nrb = seq // rr
    ncb = rr // CHUNK
    nc = seq // CHUNK
    ngc = gcs.shape[1]
    n_tiles = bsz * nrb * nh
    assert GROUP == HEAD_DIM and rr % GROUP == 0

    tile_in = lambda i: jnp.minimum(i, n_tiles - 1)
    tile_out = lambda i: jnp.maximum(i - 1, 0)
    row_of = lambda i: tile_in(i) // nh
    head_of = lambda i: tile_in(i) % nh
    in_specs = []
    for part in range(3):
        col_of = functools.partial(lambda i, part: part * nh + head_of(i), part=part)
        in_specs.append(pl.BlockSpec((rr, HEAD_DIM), functools.partial(
            lambda i, col_of: (row_of(i), col_of(i)), col_of=col_of)))
        in_specs.extend(_halo_specs(rr, HEAD_DIM, col_of, row_of, rows))
    for part in range(3):
        in_specs.append(pl.BlockSpec((CONV_QKV_WIDTH, HEAD_DIM), functools.partial(
            lambda i, part: (0, part * nh + head_of(i)), part=part)))
    in_specs.append(pl.BlockSpec((rr, ngc), lambda i: (row_of(i), 0)))
    in_specs.append(pl.BlockSpec((ngc, rr), lambda i: (0, row_of(i))))

    def out(per_block, shape_tail, dtype):
        blk = (None, N_DIR, None, per_block) + shape_tail
        zeros = (0,) * len(shape_tail)

        def imap(i):
            t = tile_out(i)
            return (t // (nrb * nh), 0, t % nh, (t // nh) % nrb) + zeros

        return (jax.ShapeDtypeStruct((bsz, N_DIR, nh, per_block * nrb) + shape_tail, dtype),
                pl.BlockSpec(blk, imap))

    outs = [out(ncb, (2 * CHUNK, HEAD_DIM), BF16), out(ncb // 2, (CHUNK + HEAD_DIM, 2 * CHUNK), BF16),
            out(ncb, (CHUNK, HEAD_DIM), BF16), out(ncb, (1, HEAD_DIM), F32)]
    return pl.pallas_call(
        functools.partial(_prep_kernel, nrb=nrb, nh=nh),
        grid=(n_tiles + 1,),
        in_specs=in_specs,
        out_specs=[o[1] for o in outs],
        out_shape=[o[0] for o in outs],
        scratch_shapes=[pltpu.VMEM((3, rr + 2 * F32_SUBLANES, HEAD_DIM), F32),
                        pltpu.VMEM((3, rr, HEAD_DIM), F32),
                        pltpu.VMEM((4, rr, HEAD_DIM), F32),
                        pltpu.VMEM((N_DIR, 1, rr), F32)],
        compiler_params=pltpu.CompilerParams(dimension_semantics=("arbitrary",),
                                             vmem_limit_bytes=VMEM_LIMIT_BYTES),
        name="prep",
    )(qkv, qkv, qkv, qkv, qkv, qkv, qkv, qkv, qkv, conv_w, conv_w, conv_w, gcs, gcst)


def _scan_kernel(l1f, l2f, uf, glf, l1b, l2b, ub, glb, of_ref, ob_ref, s_ref, *, ncb, nh):
    @pl.when(pl.program_id(1) == 0)
    def _():
        s_ref[...] = jnp.zeros_like(s_ref)

    dirs = ((l1f, l2f, uf, glf, of_ref), (l1b, l2b, ub, glb, ob_ref))

    zero_rows = jnp.zeros((CHUNK, HEAD_DIM), BF16)

    def chunk_step(pair, parity):
        chains = [(d, h, pair[d], 2 * pair[d] + parity[d]) for d in range(N_DIR) for h in range(nh)]
        states = [s_ref[d, h] for d, h, _, _ in chains]
        r1s = [_dot(dirs[d][0][h, cc], s.astype(BF16))
               for (d, h, _, cc), s in zip(chains, states)]
        v_news = [dirs[d][2][h, cc].astype(F32) - r1[:CHUNK] for (d, h, _, cc), r1 in zip(chains, r1s)]
        r2s = []
        for (d, h, pc, _), v in zip(chains, v_news):
            vb = v.astype(BF16)
            rhs = jnp.concatenate([zero_rows, vb] if parity[d] else [vb, zero_rows], axis=0)
            r2s.append(_dot(dirs[d][1][h, pc], rhs))
        for (d, h, _, cc), s, r1, r2 in zip(chains, states, r1s, r2s):
            s_ref[d, h] = s * dirs[d][3][h, cc] + r2[CHUNK:]
            r0 = pl.multiple_of(cc * CHUNK, CHUNK)
            dirs[d][4][pl.ds(r0, CHUNK), h * HEAD_DIM:(h + 1) * HEAD_DIM] = (
                r1[CHUNK:] + r2[:CHUNK]).astype(dirs[d][4].dtype)

    def body(pi, carry):
        pair = (pi, ncb // 2 - 1 - pi)
        chunk_step(pair, (0, 1))
        chunk_step(pair, (1, 0))
        return carry

    lax.fori_loop(0, ncb // 2, body, 0)


def _scan(lhs1, lhs2, u, gl, *, seq):
    bsz, _, nh, nc = lhs1.shape[:4]
    rr = SCAN_ROWS
    ncb = rr // CHUNK
    nt = nc // ncb

    def specs(d):
        def mk(arr):
            tail = arr.shape[4:]
            zeros = (0,) * len(tail)
            if d == 0:
                imap = lambda b, t: (b, 0, 0, t) + zeros
            else:
                imap = lambda b, t: (b, 1, 0, nt - 1 - t) + zeros
            return pl.BlockSpec((None, None, nh, arr.shape[3] // nt) + tail, imap)
        return [mk(a) for a in (lhs1, lhs2, u, gl)]

    o_shape = jax.ShapeDtypeStruct((bsz * seq, nh * HEAD_DIM), BF16)
    return pl.pallas_call(
        functools.partial(_scan_kernel, ncb=ncb, nh=nh),
        grid=(bsz, nt),
        in_specs=specs(0) + specs(1),
        out_specs=[pl.BlockSpec((rr, nh * HEAD_DIM), lambda b, t: (b * nt + t, 0)),
                   pl.BlockSpec((rr, nh * HEAD_DIM), lambda b, t: (b * nt + nt - 1 - t, 0))],
        out_shape=[o_shape, o_shape],
        scratch_shapes=[pltpu.VMEM((N_DIR, nh, HEAD_DIM, HEAD_DIM), F32)],
        compiler_params=pltpu.CompilerParams(dimension_semantics=("arbitrary", "arbitrary"),
                                             vmem_limit_bytes=VMEM_LIMIT_BYTES),
        name="scan",
    )(lhs1, lhs2, u, gl, lhs1, lhs2, u, gl)


def _out_kernel(x_ref, of_ref, ob_ref, zs_ref, pm, pp, pn, gz_ref, sga_ref, sgb_ref, gate_ref,
                cwa_ref, gnw_ref, fnw_ref, wpa, wpb, wo, y_ref, ext_ref, *, tiles_per_seq, nh, final):
    r = pl.program_id(0) % tiles_per_seq
    sub = x_ref.shape[0] // OUT_SUBTILES
    rows = [slice(i * sub, (i + 1) * sub) for i in range(OUT_SUBTILES)]

    pv = jnp.where(r == 0, 0.0, pp[...].astype(F32))
    nx = jnp.where(r == tiles_per_seq - 1, 0.0, pn[...].astype(F32))
    conv = _conv_rows(ext_ref, pm[...].astype(F32), pv, nx, cwa_ref, CONV_A_WIDTH)

    def yb_input(rs):
        o = of_ref[rs, :].astype(F32) + ob_ref[rs, :].astype(F32)
        heads = []
        for h in range(nh):
            oh = o[:, h * HEAD_DIM:(h + 1) * HEAD_DIM]
            heads.append(oh * lax.rsqrt(jnp.mean(oh * oh, axis=-1, keepdims=True) + NORM_EPS) * gnw_ref[...])
        return (jnp.concatenate(heads, axis=1) * zs_ref[rs, :].astype(F32)).astype(BF16)

    yb_ins = [yb_input(rs) for rs in rows]
    ya_ins = [(gz_ref[rs, :].astype(F32) * conv[rs]).astype(BF16) for rs in rows]
    ybs = [_dot(v, wpb[...]) for v in yb_ins]
    yas = [_dot(v, wpa[...]) for v in ya_ins]
    mergeds = [(sga_ref[rs, :].astype(F32) * ya + sgb_ref[rs, :].astype(F32) * yb).astype(BF16)
               for rs, ya, yb in zip(rows, yas, ybs)]
    deltas = [_dot(m, wo[...]) for m in mergeds]
    for rs, delta in zip(rows, deltas):
        xn = x_ref[rs, :] + gate_ref[0] * delta
        if final:
            xn = xn * lax.rsqrt(jnp.mean(xn * xn, axis=-1, keepdims=True) + NORM_EPS) * fnw_ref[...]
        y_ref[rs, :] = xn


def _out(x2, o_f, o_b, zs, p, gz, sga, sgb, gate, conv_a_w, gdn_norm_w, final_norm_w, wpa, wpb, wo,
         *, seq, nh, final):
    rows, d = x2.shape
    tm = OUT_ROWS
    tiles_per_seq = seq // tm
    row_blk = lambda w: pl.BlockSpec((tm, w), lambda i: (i, 0))
    const = lambda a: pl.BlockSpec(a.shape, lambda i: (0,) * a.ndim)
    halo_prev, halo_next = _halo_specs(tm, p.shape[1], lambda i: 0, lambda i: i, rows)
    return pl.pallas_call(
        functools.partial(_out_kernel, tiles_per_seq=tiles_per_seq, nh=nh, final=final),
        grid=(rows // tm,),
        in_specs=[row_blk(d), row_blk(o_f.shape[1]), row_blk(o_b.shape[1]), row_blk(zs.shape[1]),
                  row_blk(p.shape[1]), halo_prev, halo_next, row_blk(gz.shape[1]),
                  row_blk(sga.shape[1]), row_blk(sgb.shape[1]),
                  pl.BlockSpec((1, 1, d), lambda i: (i // tiles_per_seq, 0, 0)),
                  const(conv_a_w), const(gdn_norm_w), const(final_norm_w),
                  const(wpa), const(wpb), const(wo)],
        out_specs=row_blk(d),
        out_shape=jax.ShapeDtypeStruct((rows, d), F32),
        scratch_shapes=[pltpu.VMEM((tm + 2 * F32_SUBLANES, p.shape[1]), F32)],
        compiler_params=pltpu.CompilerParams(dimension_semantics=("arbitrary",),
                                             vmem_limit_bytes=VMEM_LIMIT_BYTES),
        name="out",
    )(x2, o_f, o_b, zs, p, p, p, gz, sga, sgb, gate, conv_a_w, gdn_norm_w, final_norm_w, wpa, wpb, wo)


def kernel(x, c, w_ada, b_ada, norm_w, w_in, conv_a_w, conv_qkv_w, a_log, dt_bias, gdn_norm_w,
           w_pa, w_pb, w_o, final_norm_w):
    bsz, seq, d = x.shape
    depth = w_ada.shape[0]
    nh = a_log.shape[2]
    d_conv = conv_a_w.shape[2]
    key_dim = nh * HEAD_DIM
    assert conv_qkv_w.shape[2] == 3 * key_dim and gdn_norm_w.shape[1] == HEAD_DIM
    assert seq % PREP_ROWS == 0 and seq % SCAN_ROWS == 0 and seq % OUT_ROWS == 0
    n_gate = N_DIR * nh
    widths = dict(bg=d_conv, cg=d_conv, ax=d_conv, az=d_conv, qkv=3 * key_dim, z=key_dim,
                  ab=2 * n_gate, ga=d, gb=d)
    assert sum(widths.values()) == w_in.shape[2]
    offs, start = {}, 0
    for g, wd in widths.items():
        offs[g] = (0, start)
        start += wd
    gates_start = offs["ga"][1]
    offs["ga"], offs["gb"] = (1, 0), (1, widths["ga"])
    assert all(o % HEAD_DIM == 0 for _, o in offs.values())

    x2 = x.reshape(bsz * seq, d)
    c_pad = jnp.zeros((8, d), F32).at[:bsz].set(c)
    for l in range(depth):
        mod = _ada(c_pad, w_ada[l], b_ada[l][None, :])[:bsz]
        shift, scale, gate = (mod[:, i * d:(i + 1) * d][:, None, :] for i in range(3))

        w_main = w_in[l].astype(BF16)
        w_gates = w_in[l][:, gates_start:].astype(BF16)
        zeros = jnp.zeros((1, n_gate), F32)
        alog = jnp.concatenate([a_log[l].reshape(1, n_gate), zeros], axis=1)
        dtb = jnp.concatenate([dt_bias[l].reshape(1, n_gate), zeros], axis=1)
        p, gz, qkv, zs, sga, sgb, gcs, gcst = _inproj(
            x2, scale, shift, norm_w[l][None, :], w_main, w_gates, offs, widths, alog, dtb, seq=seq, nh=nh)

        lhs1, lhs2, u, gl = _prep(qkv, conv_qkv_w[l], gcs, gcst, bsz=bsz, seq=seq, nh=nh)
        o_f, o_b = _scan(lhs1, lhs2, u, gl, seq=seq)

        x2 = _out(x2, o_f, o_b, zs, p, gz, sga, sgb, gate, conv_a_w[l], gdn_norm_w[l][None, :],
                  final_norm_w[None, :], w_pa[l].astype(BF16), w_pb[l].astype(BF16), w_o[l].astype(BF16),
                  seq=seq, nh=nh, final=(l == depth - 1))
    return x2.reshape(bsz, seq, d)
```

```python
import functools

import jax
import jax.numpy as jnp
from jax import lax
from jax.experimental import pallas as pl
from jax.experimental.pallas import tpu as pltpu

F32 = jnp.float32
BF16 = jnp.bfloat16

HEAD_DIM = 128
CHUNK = 64
N_DIR = 2
CONV_A_WIDTH = 3
CONV_QKV_WIDTH = 5
NORM_EPS = 1e-6
L2_EPS = 1e-6
LOG2_E = 1.4426950408889634
BF16_SUBLANES = 16
F32_SUBLANES = 8
VMEM_LIMIT_BYTES = 56 * 1024 * 1024

INPROJ_ROWS = 512
INPROJ_COLS = 512
PREP_ROWS = 1024
SCAN_ROWS = 512
OUT_ROWS = 512
OUT_SUBTILES = 2


def _dot(a, b):
    return jnp.dot(a, b, preferred_element_type=F32)


def _dot_nt(a, b):
    return lax.dot_general(a, b, (((1,), (1,)), ((), ())), preferred_element_type=F32)


def _silu(x):
    return x * jax.nn.sigmoid(x)


def _ada_kernel(c_ref, w_ref, b_ref, o_ref):
    sc = _silu(c_ref[...])
    o_ref[...] = jnp.dot(sc, w_ref[...], preferred_element_type=F32,
                         precision=lax.Precision.HIGHEST) + b_ref[...]


def _ada(c_pad, w, b):
    m, d = c_pad.shape
    n = w.shape[1]
    bn = d
    return pl.pallas_call(
        _ada_kernel,
        grid=(n // bn,),
        in_specs=[pl.BlockSpec((m, d), lambda j: (0, 0)),
                  pl.BlockSpec((d, bn), lambda j: (0, j)),
                  pl.BlockSpec((1, bn), lambda j: (0, j))],
        out_specs=pl.BlockSpec((m, bn), lambda j: (0, j)),
        out_shape=jax.ShapeDtypeStruct((m, n), F32),
        name="ada",
    )(c_pad, w, b)


def _seg_cumsum(x, axis, reverse):
    n = x.shape[axis]
    pos = lax.broadcasted_iota(jnp.int32, x.shape, axis) % CHUNK
    s = 1
    while s < CHUNK:
        if reverse:
            x = x + jnp.where(pos < CHUNK - s, pltpu.roll(x, n - s, axis), 0.0)
        else:
            x = x + jnp.where(pos >= s, pltpu.roll(x, s, axis), 0.0)
        s *= 2
    return x


def _decay_and_beta(ab, alog, dtb, axis, nh):
    pre = ab + dtb
    softplus = jnp.maximum(pre, 0.0) + jnp.log(1.0 + jnp.exp(-jnp.abs(pre)))
    g = -jnp.exp(alog) * softplus * LOG2_E
    beta = jax.nn.sigmoid(ab)
    col = lax.broadcasted_iota(jnp.int32, ab.shape, 1 - axis)
    gf = _seg_cumsum(g, axis, False)
    gb = _seg_cumsum(g, axis, True)
    return jnp.where(col < nh, gf, jnp.where(col < N_DIR * nh, gb, beta))


def _inproj_kernel(x_ref, sc_ref, sh_ref, nw_ref, w_ref, wg_ref, wab_ref, wabt, alog, dtb, alogt, dtbt,
                   p_ref, gz_ref, qkv_ref, zs_ref, sga_ref, sgb_ref, gcs_ref, gcst_ref, *, nh, cw, offs):
    x = x_ref[...]
    h = x * lax.rsqrt(jnp.mean(x * x, axis=-1, keepdims=True) + NORM_EPS) * nw_ref[...]
    h = h * (1.0 + sc_ref[0]) + sh_ref[0]
    hb = h.astype(BF16)

    def proj(group, c0, width):
        which, start = offs[group]
        ref = (w_ref, wg_ref, wab_ref)[which]
        return _dot(hb, ref[:, start + c0:start + c0 + width])

    for c0 in range(0, p_ref.shape[1], cw):
        sl = slice(c0, c0 + cw)
        p_ref[:, sl] = (proj("bg", c0, cw) * proj("ax", c0, cw)).astype(BF16)
        gz_ref[:, sl] = (proj("cg", c0, cw) * _silu(proj("az", c0, cw))).astype(BF16)
    for c0 in range(0, qkv_ref.shape[1], cw):
        qkv_ref[:, c0:c0 + cw] = proj("qkv", c0, cw).astype(BF16)
    for c0 in range(0, zs_ref.shape[1], cw):
        zs_ref[:, c0:c0 + cw] = _silu(proj("z", c0, cw)).astype(BF16)
    for c0 in range(0, sga_ref.shape[1], cw):
        sga_ref[:, c0:c0 + cw] = jax.nn.sigmoid(proj("ga", c0, cw)).astype(BF16)
        sgb_ref[:, c0:c0 + cw] = jax.nn.sigmoid(proj("gb", c0, cw)).astype(BF16)

    gcs_ref[...] = _decay_and_beta(proj("ab", 0, gcs_ref.shape[1]), alog[...], dtb[...], 0, nh)
    gcst_ref[...] = _decay_and_beta(_dot_nt(wabt[...], hb), alogt[...], dtbt[...], 1, nh)


def _inproj(x2, scale, shift, norm_w, w_main, w_gates, w_ab, offs, widths, alog, dtb, *, seq, nh):
    rows, d = x2.shape
    tm = INPROJ_ROWS
    assert seq % tm == 0 and tm % CHUNK == 0
    tiles_per_seq = seq // tm
    nab = widths["ab"]
    resident = lambda a: pl.BlockSpec(a.shape, lambda i: (0,) * a.ndim, pipeline_mode=pl.Buffered(1))
    consts = (norm_w, w_main, w_gates, w_ab, w_ab.T, alog, dtb, alog.T, dtb.T)
    row_blk = lambda w: pl.BlockSpec((tm, w), lambda i: (i, 0))
    mod_blk = pl.BlockSpec((1, 1, d), lambda i: (i // tiles_per_seq, 0, 0))
    out_shapes = [jax.ShapeDtypeStruct((rows, widths[g]), BF16) for g in ("bg", "cg", "qkv", "z", "ga", "gb")]
    out_shapes += [jax.ShapeDtypeStruct((rows, nab), F32), jax.ShapeDtypeStruct((nab, rows), F32)]
    out_specs = [row_blk(s.shape[1]) for s in out_shapes[:-1]]
    out_specs += [pl.BlockSpec((nab, tm), lambda i: (0, i))]
    return pl.pallas_call(
        functools.partial(_inproj_kernel, nh=nh, cw=INPROJ_COLS, offs=offs),
        grid=(rows // tm,),
        in_specs=[row_blk(d), mod_blk, mod_blk] + [resident(a) for a in consts],
        out_specs=out_specs,
        out_shape=out_shapes,
        compiler_params=pltpu.CompilerParams(dimension_semantics=("arbitrary",),
                                             vmem_limit_bytes=VMEM_LIMIT_BYTES),
        name="inproj",
    )(x2, scale, shift, *consts)


def _conv_rows(ext_ref, xm, prev, nxt, w_ref, width):
    r = xm.shape[0]
    half = width // 2
    pad = F32_SUBLANES
    ext_ref[pad:pad + r, :] = xm
    ext_ref[pad - half:pad, :] = prev[prev.shape[0] - half:, :]
    ext_ref[pad + r:pad + r + half, :] = nxt[:half, :]
    acc = None
    for t in range(width):
        term = ext_ref[pad - half + t:pad - half + t + r, :] * w_ref[t:t + 1, :]
        acc = term if acc is None else acc + term
    return acc


def _halo_specs(tile_rows, width_blk, col_of, row_tile_of, total_rows):
    per = tile_rows // BF16_SUBLANES
    last = total_rows // BF16_SUBLANES - 1

    def prev_map(*g):
        return (jnp.maximum(row_tile_of(*g) * per - 1, 0), col_of(*g))

    def next_map(*g):
        return (jnp.minimum((row_tile_of(*g) + 1) * per, last), col_of(*g))

    return (pl.BlockSpec((BF16_SUBLANES, width_blk), prev_map),
            pl.BlockSpec((BF16_SUBLANES, width_blk), next_map))


GROUP = 128


def _half_rows(a, b, second):
    start = b if second else 0
    return jnp.concatenate([a[k + start:k + start + b] for k in range(0, a.shape[0], 2 * b)], axis=0)


def _spread_rows(c, b, second, other):
    parts = []
    for k in range(0, c.shape[0], b):
        pair = (other[k:k + b], c[k:k + b])
        parts.extend(pair if second else pair[::-1])
    return jnp.concatenate(parts, axis=0)


def _zero_row_after(v):
    bits = pltpu.bitcast(v[:F32_SUBLANES, :HEAD_DIM], jnp.int32)
    zero = lax.shift_right_logical(lax.shift_right_logical(bits, 16), 16)
    return zero[:1].astype(F32)


def _block_inverses(pmats, lowers, ii, jj, fillers=()):
    same_block = lambda s: (ii >> s) == (jj >> s)
    eye = jnp.where(ii == jj, 1.0, 0.0)
    fillers = list(fillers)

    def fill(after):
        if fillers:
            fillers.pop(0)(_zero_row_after(after))

    base = F32_SUBLANES.bit_length() - 1
    ps = [jnp.where(same_block(base), p, 0.0) for p in pmats]
    xs = [eye + p for p in ps]
    pbs = [p.astype(BF16) for p in ps]
    p2s = [_dot(pb, pb) for pb in pbs]
    fill(p2s[-1])
    ys = [_dot(p2.astype(BF16), jnp.concatenate([x, p2], axis=1).astype(BF16)) for x, p2 in zip(xs, p2s)]
    xs = [x + y[:, :GROUP] for x, y in zip(xs, ys)]
    fill(ys[-1])
    xs = [x + _dot(y[:, GROUP:].astype(BF16), x.astype(BF16)) for x, y in zip(xs, ys)]

    zeros = jnp.zeros((GROUP // 2, GROUP), F32)
    for s in range(base, CHUNK.bit_length() - 1):
        b = 1 << s
        off = same_block(s + 1) & jnp.logical_not(same_block(s))
        offs = [jnp.where(off, p, 0.0) for p in pmats]
        xbs = [x.astype(BF16) for x in xs]
        ys = [_dot(_half_rows(o, b, lo).astype(BF16), xb) for o, xb, lo in zip(offs, xbs, lowers)]
        ups = [_dot(_half_rows(x, b, lo).astype(BF16), _spread_rows(y, b, lo, zeros).astype(BF16))
               for x, y, lo in zip(xs, ys, lowers)]
        xs = [_spread_rows(_half_rows(x, b, lo) + up, b, lo, _half_rows(x, b, not lo))
              for x, up, lo in zip(xs, ups, lowers)]
        fill(ys[-1])
    assert not fillers
    return xs


def _prep_kernel(qm, qp, qn, km, kp, kn, vm, vp, vn, cwq, cwk, cwv, gcs_ref, gcst_ref,
                 lhs1_ref, lhs2_ref, u_ref, gl_ref, ext_ref, qkv_s, col_s, row_s, *, nrb, nh):
    i = pl.program_id(0)
    j = jnp.minimum(i, pl.num_programs(0) - 2)
    r = (j // nh) % nrb
    h = j % nh
    first = r == 0
    last = r == nrb - 1

    @pl.when(i == 0)
    def _():
        qkv_s[...] = jnp.zeros_like(qkv_s)
        col_s[...] = jnp.zeros_like(col_s)
        row_s[...] = jnp.zeros_like(row_s)

    q, k, v = qkv_s[0], qkv_s[1], qkv_s[2]
    g_col = (col_s[0], col_s[1])
    b_col = (col_s[2], col_s[3])
    nbc = (-b_col[0], -b_col[1])
    g_row = (row_s[0], row_s[1])

    def conv_silu(idx, m, p, n, cw, zero_row):
        pv = jnp.where(first, 0.0, p[...].astype(F32))
        nx = jnp.where(last, 0.0, n[...].astype(F32))
        w = cw[...] + zero_row
        return _silu(_conv_rows(ext_ref.at[idx], m[...].astype(F32), pv, nx, w, CONV_QKV_WIDTH))

    def stage1_q(zero_row):
        q1 = conv_silu(0, qm, qp, qn, cwq, zero_row)
        qkv_s[0] = q1 * (lax.rsqrt(jnp.sum(q1 * q1, axis=-1, keepdims=True) + L2_EPS) * (HEAD_DIM ** -0.5))

    def stage1_k(zero_row):
        k1 = conv_silu(1, km, kp, kn, cwk, zero_row)
        qkv_s[1] = k1 * lax.rsqrt(jnp.sum(k1 * k1, axis=-1, keepdims=True) + L2_EPS)

    def stage1_v(zero_row):
        qkv_s[2] = conv_silu(2, vm, vp, vn, cwv, zero_row)

    def stage1_gates(zero_row):
        del zero_row
        gcs = gcs_ref[...]
        lane = lax.broadcasted_iota(jnp.int32, gcs.shape, 1)
        for slot in range(4):
            picked = jnp.sum(jnp.where(lane == slot * nh + h, gcs, 0.0), axis=1, keepdims=True)
            col_s[slot] = jnp.broadcast_to(picked, col_s.shape[1:])
        for d in range(N_DIR):
            row_s[d] = gcst_ref[pl.ds(d * nh + h, 1), :]

    ii = lax.broadcasted_iota(jnp.int32, (GROUP, GROUP), 0)
    jj = lax.broadcasted_iota(jnp.int32, (GROUP, GROUP), 1)
    same_chunk = (ii // CHUNK) == (jj // CHUNK)
    cpg = GROUP // CHUNK

    inst = []
    for gi in range(q.shape[0] // GROUP):
        rs = slice(gi * GROUP, (gi + 1) * GROUP)
        kc, qc, vc = k[rs], q[rs], v[rs]
        kq = _dot_nt(jnp.concatenate([kc, qc], axis=0).astype(BF16), kc.astype(BF16))
        kk, qk = kq[:GROUP], kq[GROUP:]
        for d in range(N_DIR):
            gc = g_col[d][rs]
            gr = g_row[d][:, rs]
            bc = b_col[d][rs]
            incl = same_chunk & ((ii >= jj) if d == 0 else (ii <= jj))
            strict = same_chunk & ((ii > jj) if d == 0 else (ii < jj))
            decay = jnp.exp2(gc - gr)
            pmat = jnp.where(strict, (nbc[d][rs] * kk) * decay, 0.0)
            aqk = jnp.where(incl, qk * decay, 0.0)
            eg = jnp.exp2(gc)
            edge = CHUNK - 1 if d == 0 else 0
            g_tot = jnp.concatenate(
                [jnp.broadcast_to(gc[c * CHUNK + edge:c * CHUNK + edge + 1], (CHUNK, HEAD_DIM))
                 for c in range(cpg)], axis=0)
            kdt = (kc * jnp.exp2(g_tot - gc)).T
            qd = qc * eg
            for c in range(cpg):
                cs = slice(c * CHUNK, (c + 1) * CHUNK)
                lhs1_ref[d, gi * cpg + c, CHUNK:, :] = qd[cs].astype(BF16)
                gl_ref[d, gi * cpg + c] = jnp.exp2(g_tot[c * CHUNK:c * CHUNK + 1])
            for p in range(cpg // 2):
                ls = slice(p * 2 * CHUNK, (p + 1) * 2 * CHUNK)
                a_pair = (aqk[p * 2 * CHUNK:p * 2 * CHUNK + CHUNK, ls]
                          + aqk[p * 2 * CHUNK + CHUNK:(p + 1) * 2 * CHUNK, ls])
                lhs2_ref[d, gi * (cpg // 2) + p] = jnp.concatenate([a_pair, kdt[:, ls]], axis=0).astype(BF16)
            rhs = jnp.concatenate([vc * bc, kc * (bc * eg)], axis=1).astype(BF16)
            inst.append(dict(gi=gi, d=d, pmat=pmat, rhs=rhs))

    tmats = _block_inverses([it["pmat"] for it in inst], [it["d"] == 0 for it in inst], ii, jj,
                            fillers=[stage1_gates, stage1_q, stage1_k, stage1_v])

    for it, tmat in zip(inst, tmats):
        gi, d = it["gi"], it["d"]
        uw = _dot(tmat.astype(BF16), it["rhs"])
        for c in range(cpg):
            cs = slice(c * CHUNK, (c + 1) * CHUNK)
            lhs1_ref[d, gi * cpg + c, :CHUNK, :] = uw[cs, HEAD_DIM:].astype(BF16)
            u_ref[d, gi * cpg + c] = uw[cs, :HEAD_DIM].astype(BF16)


def _prep(qkv, conv_w, gcs, gcst, *, bsz, seq, nh):
    rows = qkv.shape[0]
    rr = PREP_ROWS
    nrb = seq // rr
    ncb = rr // CHUNK
    nc = seq // CHUNK
    ngc = gcs.shape[1]
    n_tiles = bsz * nrb * nh
    assert GROUP == HEAD_DIM and rr % GROUP == 0

    tile_in = lambda i: jnp.minimum(i, n_tiles - 1)
    tile_out = lambda i: jnp.maximum(i - 1, 0)
    row_of = lambda i: tile_in(i) // nh
    head_of = lambda i: tile_in(i) % nh
    in_specs = []
    for part in range(3):
        col_of = functools.partial(lambda i, part: part * nh + head_of(i), part=part)
        in_specs.append(pl.BlockSpec((rr, HEAD_DIM), functools.partial(
            lambda i, col_of: (row_of(i), col_of(i)), col_of=col_of)))
        in_specs.extend(_halo_specs(rr, HEAD_DIM, col_of, row_of, rows))
    for part in range(3):
        in_specs.append(pl.BlockSpec((CONV_QKV_WIDTH, HEAD_DIM), functools.partial(
            lambda i, part: (0, part * nh + head_of(i)), part=part)))
    in_specs.append(pl.BlockSpec((rr, ngc), lambda i: (row_of(i), 0)))
    in_specs.append(pl.BlockSpec((ngc, rr), lambda i: (0, row_of(i))))

    def out(per_block, shape_tail, dtype):
        blk = (None, N_DIR, None, per_block) + shape_tail
        zeros = (0,) * len(shape_tail)

        def imap(i):
            t = tile_out(i)
            return (t // (nrb * nh), 0, t % nh, (t // nh) % nrb) + zeros

        return (jax.ShapeDtypeStruct((bsz, N_DIR, nh, per_block * nrb) + shape_tail, dtype),
                pl.BlockSpec(blk, imap))

    outs = [out(ncb, (2 * CHUNK, HEAD_DIM), BF16), out(ncb // 2, (CHUNK + HEAD_DIM, 2 * CHUNK), BF16),
            out(ncb, (CHUNK, HEAD_DIM), BF16), out(ncb, (1, HEAD_DIM), F32)]
    return pl.pallas_call(
        functools.partial(_prep_kernel, nrb=nrb, nh=nh),
        grid=(n_tiles + 1,),
        in_specs=in_specs,
        out_specs=[o[1] for o in outs],
        out_shape=[o[0] for o in outs],
        scratch_shapes=[pltpu.VMEM((3, rr + 2 * F32_SUBLANES, HEAD_DIM), F32),
                        pltpu.VMEM((3, rr, HEAD_DIM), F32),
                        pltpu.VMEM((4, rr, HEAD_DIM), F32),
                        pltpu.VMEM((N_DIR, 1, rr), F32)],
        compiler_params=pltpu.CompilerParams(dimension_semantics=("arbitrary",),
                                             vmem_limit_bytes=VMEM_LIMIT_BYTES),
        name="prep",
    )(qkv, qkv, qkv, qkv, qkv, qkv, qkv, qkv, qkv, conv_w, conv_w, conv_w, gcs, gcst)


def _scan_kernel(l1f, l2f, uf, glf, l1b, l2b, ub, glb, of_ref, ob_ref, s_ref, *, ncb, nh):
    @pl.when(pl.program_id(1) == 0)
    def _():
        s_ref[...] = jnp.zeros_like(s_ref)

    dirs = ((l1f, l2f, uf, glf, of_ref), (l1b, l2b, ub, glb, ob_ref))

    zero_rows = jnp.zeros((CHUNK, HEAD_DIM), BF16)

    def chunk_step(pair, parity):
        chains = [(d, h, pair[d], 2 * pair[d] + parity[d]) for d in range(N_DIR) for h in range(nh)]
        states = [s_ref[d, h] for d, h, _, _ in chains]
        r1s = [_dot(dirs[d][0][h, cc], s.astype(BF16))
               for (d, h, _, cc), s in zip(chains, states)]
        v_news = [dirs[d][2][h, cc].astype(F32) - r1[:CHUNK] for (d, h, _, cc), r1 in zip(chains, r1s)]
        r2s = []
        for (d, h, pc, _), v in zip(chains, v_news):
            vb = v.astype(BF16)
            rhs = jnp.concatenate([zero_rows, vb] if parity[d] else [vb, zero_rows], axis=0)
            r2s.append(_dot(dirs[d][1][h, pc], rhs))
        for (d, h, _, cc), s, r1, r2 in zip(chains, states, r1s, r2s):
            s_ref[d, h] = s * dirs[d][3][h, cc] + r2[CHUNK:]
            r0 = pl.multiple_of(cc * CHUNK, CHUNK)
            dirs[d][4][pl.ds(r0, CHUNK), h * HEAD_DIM:(h + 1) * HEAD_DIM] = (
                r1[CHUNK:] + r2[:CHUNK]).astype(dirs[d][4].dtype)

    def body(pi, carry):
        pair = (pi, ncb // 2 - 1 - pi)
        chunk_step(pair, (0, 1))
        chunk_step(pair, (1, 0))
        return carry

    lax.fori_loop(0, ncb // 2, body, 0)


def _scan(lhs1, lhs2, u, gl, *, seq):
    bsz, _, nh, nc = lhs1.shape[:4]
    rr = SCAN_ROWS
    ncb = rr // CHUNK
    nt = nc // ncb

    def specs(d):
        def mk(arr):
            tail = arr.shape[4:]
            zeros = (0,) * len(tail)
            if d == 0:
                imap = lambda b, t: (b, 0, 0, t) + zeros
            else:
                imap = lambda b, t: (b, 1, 0, nt - 1 - t) + zeros
            return pl.BlockSpec((None, None, nh, arr.shape[3] // nt) + tail, imap)
        return [mk(a) for a in (lhs1, lhs2, u, gl)]

    o_shape = jax.ShapeDtypeStruct((bsz * seq, nh * HEAD_DIM), BF16)
    return pl.pallas_call(
        functools.partial(_scan_kernel, ncb=ncb, nh=nh),
        grid=(bsz, nt),
        in_specs=specs(0) + specs(1),
        out_specs=[pl.BlockSpec((rr, nh * HEAD_DIM), lambda b, t: (b * nt + t, 0)),
                   pl.BlockSpec((rr, nh * HEAD_DIM), lambda b, t: (b * nt + nt - 1 - t, 0))],
        out_shape=[o_shape, o_shape],
        scratch_shapes=[pltpu.VMEM((N_DIR, nh, HEAD_DIM, HEAD_DIM), F32)],
        compiler_params=pltpu.CompilerParams(dimension_semantics=("arbitrary", "arbitrary"),
                                             vmem_limit_bytes=VMEM_LIMIT_BYTES),
        name="scan",
    )(lhs1, lhs2, u, gl, lhs1, lhs2, u, gl)


def _out_kernel(x_ref, of_ref, ob_ref, zs_ref, pm, pp, pn, gz_ref, sga_ref, sgb_ref, gate_ref,
                cwa_ref, gnw_ref, fnw_ref, wpa, wpb, wo, y_ref, ext_ref, *, tiles_per_seq, nh, final):
    r = pl.program_id(0) % tiles_per_seq
    sub = x_ref.shape[0] // OUT_SUBTILES
    rows = [slice(i * sub, (i + 1) * sub) for i in range(OUT_SUBTILES)]

    pv = jnp.where(r == 0, 0.0, pp[...].astype(F32))
    nx = jnp.where(r == tiles_per_seq - 1, 0.0, pn[...].astype(F32))
    conv = _conv_rows(ext_ref, pm[...].astype(F32), pv, nx, cwa_ref, CONV_A_WIDTH)

    def yb_input(rs):
        o = of_ref[rs, :].astype(F32) + ob_ref[rs, :].astype(F32)
        heads = []
        for h in range(nh):
            oh = o[:, h * HEAD_DIM:(h + 1) * HEAD_DIM]
            heads.append(oh * lax.rsqrt(jnp.mean(oh * oh, axis=-1, keepdims=True) + NORM_EPS) * gnw_ref[...])
        return (jnp.concatenate(heads, axis=1) * zs_ref[rs, :].astype(F32)).astype(BF16)

    yb_ins = [yb_input(rs) for rs in rows]
    ya_ins = [(gz_ref[rs, :].astype(F32) * conv[rs]).astype(BF16) for rs in rows]
    ybs = [_dot(v, wpb[...]) for v in yb_ins]
    yas = [_dot(v, wpa[...]) for v in ya_ins]
    mergeds = [(sga_ref[rs, :].astype(F32) * ya + sgb_ref[rs, :].astype(F32) * yb).astype(BF16)
               for rs, ya, yb in zip(rows, yas, ybs)]
    deltas = [_dot(m, wo[...]) for m in mergeds]
    for rs, delta in zip(rows, deltas):
        xn = x_ref[rs, :] + gate_ref[0] * delta
        if final:
            xn = xn * lax.rsqrt(jnp.mean(xn * xn, axis=-1, keepdims=True) + NORM_EPS) * fnw_ref[...]
        y_ref[rs, :] = xn


def _out(x2, o_f, o_b, zs, p, gz, sga, sgb, gate, conv_a_w, gdn_norm_w, final_norm_w, wpa, wpb, wo,
         *, seq, nh, final):
    rows, d = x2.shape
    tm = OUT_ROWS
    tiles_per_seq = seq // tm
    row_blk = lambda w: pl.BlockSpec((tm, w), lambda i: (i, 0))
    const = lambda a: pl.BlockSpec(a.shape, lambda i: (0,) * a.ndim)
    halo_prev, halo_next = _halo_specs(tm, p.shape[1], lambda i: 0, lambda i: i, rows)
    return pl.pallas_call(
        functools.partial(_out_kernel, tiles_per_seq=tiles_per_seq, nh=nh, final=final),
        grid=(rows // tm,),
        in_specs=[row_blk(d), row_blk(o_f.shape[1]), row_blk(o_b.shape[1]), row_blk(zs.shape[1]),
                  row_blk(p.shape[1]), halo_prev, halo_next, row_blk(gz.shape[1]),
                  row_blk(sga.shape[1]), row_blk(sgb.shape[1]),
                  pl.BlockSpec((1, 1, d), lambda i: (i // tiles_per_seq, 0, 0)),
                  const(conv_a_w), const(gdn_norm_w), const(final_norm_w),
                  const(wpa), const(wpb), const(wo)],
        out_specs=row_blk(d),
        out_shape=jax.ShapeDtypeStruct((rows, d), F32),
        scratch_shapes=[pltpu.VMEM((tm + 2 * F32_SUBLANES, p.shape[1]), F32)],
        compiler_params=pltpu.CompilerParams(dimension_semantics=("arbitrary",),
                                             vmem_limit_bytes=VMEM_LIMIT_BYTES),
        name="out",
    )(x2, o_f, o_b, zs, p, p, p, gz, sga, sgb, gate, conv_a_w, gdn_norm_w, final_norm_w, wpa, wpb, wo)


def kernel(x, c, w_ada, b_ada, norm_w, w_in, conv_a_w, conv_qkv_w, a_log, dt_bias, gdn_norm_w,
           w_pa, w_pb, w_o, final_norm_w):
    bsz, seq, d = x.shape
    depth = w_ada.shape[0]
    nh = a_log.shape[2]
    d_conv = conv_a_w.shape[2]
    key_dim = nh * HEAD_DIM
    assert conv_qkv_w.shape[2] == 3 * key_dim and gdn_norm_w.shape[1] == HEAD_DIM
    assert seq % PREP_ROWS == 0 and seq % SCAN_ROWS == 0 and seq % OUT_ROWS == 0
    n_gate = N_DIR * nh
    widths = dict(bg=d_conv, cg=d_conv, ax=d_conv, az=d_conv, qkv=3 * key_dim, z=key_dim,
                  ab=2 * n_gate, ga=d, gb=d)
    assert sum(widths.values()) == w_in.shape[2]
    offs, start = {}, 0
    for g, wd in widths.items():
        offs[g] = (0, start)
        start += wd
    ab_start, gates_start = offs["ab"][1], offs["ga"][1]
    offs["ab"], offs["ga"], offs["gb"] = (2, 0), (1, 0), (1, widths["ga"])
    assert all(o % HEAD_DIM == 0 for _, o in offs.values()) and ab_start % HEAD_DIM == 0

    x2 = x.reshape(bsz * seq, d)
    c_pad = jnp.zeros((8, d), F32).at[:bsz].set(c)
    for l in range(depth):
        mod = _ada(c_pad, w_ada[l], b_ada[l][None, :])[:bsz]
        shift, scale, gate = (mod[:, i * d:(i + 1) * d][:, None, :] for i in range(3))

        w_main = w_in[l][:, :ab_start].astype(BF16)
        w_ab = w_in[l][:, ab_start:gates_start].astype(BF16)
        w_gates = w_in[l][:, gates_start:].astype(BF16)
        zeros = jnp.zeros((1, n_gate), F32)
        alog = jnp.concatenate([a_log[l].reshape(1, n_gate), zeros], axis=1)
        dtb = jnp.concatenate([dt_bias[l].reshape(1, n_gate), zeros], axis=1)
        p, gz, qkv, zs, sga, sgb, gcs, gcst = _inproj(
            x2, scale, shift, norm_w[l][None, :], w_main, w_gates, w_ab, offs, widths, alog, dtb,
            seq=seq, nh=nh)

        lhs1, lhs2, u, gl = _prep(qkv, conv_qkv_w[l], gcs, gcst, bsz=bsz, seq=seq, nh=nh)
        o_f, o_b = _scan(lhs1, lhs2, u, gl, seq=seq)

        x2 = _out(x2, o_f, o_b, zs, p, gz, sga, sgb, gate, conv_a_w[l], gdn_norm_w[l][None, :],
                  final_norm_w[None, :], w_pa[l].astype(BF16), w_pb[l].astype(BF16), w_o[l].astype(BF16),
                  seq=seq, nh=nh, final=(l == depth - 1))
    return x2.reshape(bsz, seq, d)
```

```python
import functools

import jax
import jax.numpy as jnp
from jax import lax
from jax.experimental import pallas as pl
from jax.experimental.pallas import tpu as pltpu

F32 = jnp.float32
BF16 = jnp.bfloat16

HEAD_DIM = 128
CHUNK = 64
N_DIR = 2
CONV_A_WIDTH = 3
CONV_QKV_WIDTH = 5
NORM_EPS = 1e-6
L2_EPS = 1e-6
LOG2_E = 1.4426950408889634
BF16_SUBLANES = 16
F32_SUBLANES = 8
VMEM_LIMIT_BYTES = 56 * 1024 * 1024

INPROJ_ROWS = 512
INPROJ_COLS = 512
PREP_ROWS = 1024
SCAN_ROWS = 512
OUT_ROWS = 512
OUT_SUBTILES = 2


def _dot(a, b):
    return jnp.dot(a, b, preferred_element_type=F32)


def _dot_nt(a, b):
    return lax.dot_general(a, b, (((1,), (1,)), ((), ())), preferred_element_type=F32)


def _silu(x):
    return x * jax.nn.sigmoid(x)


def _ada_kernel(c_ref, w_ref, b_ref, o_ref):
    sc = _silu(c_ref[...])
    o_ref[...] = jnp.dot(sc, w_ref[...], preferred_element_type=F32,
                         precision=lax.Precision.HIGHEST) + b_ref[...]


def _ada(c_pad, w, b):
    m, d = c_pad.shape
    n = w.shape[1]
    bn = d
    return pl.pallas_call(
        _ada_kernel,
        grid=(n // bn,),
        in_specs=[pl.BlockSpec((m, d), lambda j: (0, 0)),
                  pl.BlockSpec((d, bn), lambda j: (0, j)),
                  pl.BlockSpec((1, bn), lambda j: (0, j))],
        out_specs=pl.BlockSpec((m, bn), lambda j: (0, j)),
        out_shape=jax.ShapeDtypeStruct((m, n), F32),
        name="ada",
    )(c_pad, w, b)


CAST_COLS = 1024


def _cast_kernel(w_ref, o_ref):
    o_ref[...] = w_ref[...].astype(o_ref.dtype)


def _cast_leading_cols(w3, layer, ncols):
    d = w3.shape[1]
    assert ncols % CAST_COLS == 0
    return pl.pallas_call(
        _cast_kernel,
        grid=(ncols // CAST_COLS,),
        in_specs=[pl.BlockSpec((None, d, CAST_COLS), lambda j: (layer, 0, j))],
        out_specs=pl.BlockSpec((d, CAST_COLS), lambda j: (0, j)),
        out_shape=jax.ShapeDtypeStruct((d, ncols), BF16),
        compiler_params=pltpu.CompilerParams(dimension_semantics=("arbitrary",),
                                             vmem_limit_bytes=VMEM_LIMIT_BYTES),
        name="cast",
    )(w3)


def _cast_tail_kernel(a_ref, b_ref, head_ref, tail_ref, *, skip):
    lanes = a_ref.shape[1]

    @pl.when(pl.program_id(0) == 0)
    def _():
        head_ref[...] = a_ref[:, :skip].astype(head_ref.dtype)

    lane = lax.broadcasted_iota(jnp.int32, a_ref.shape, 1)
    shifted = jnp.where(lane < lanes - skip,
                        pltpu.roll(a_ref[...], lanes - skip, 1), pltpu.roll(b_ref[...], lanes - skip, 1))
    tail_ref[...] = shifted.astype(tail_ref.dtype)


def _cast_tail_cols(w3, layer, start, skip, ncols):
    d = w3.shape[1]
    assert start % HEAD_DIM == 0 and ncols % HEAD_DIM == 0 and 0 < skip < HEAD_DIM
    assert start + skip + ncols == w3.shape[2]
    first = start // HEAD_DIM
    return pl.pallas_call(
        functools.partial(_cast_tail_kernel, skip=skip),
        grid=(ncols // HEAD_DIM,),
        in_specs=[pl.BlockSpec((None, d, HEAD_DIM), lambda j: (layer, 0, first + j)),
                  pl.BlockSpec((None, d, HEAD_DIM), lambda j: (layer, 0, first + j + 1))],
        out_specs=[pl.BlockSpec((d, skip), lambda j: (0, 0)),
                   pl.BlockSpec((d, HEAD_DIM), lambda j: (0, j))],
        out_shape=[jax.ShapeDtypeStruct((d, skip), BF16), jax.ShapeDtypeStruct((d, ncols), BF16)],
        compiler_params=pltpu.CompilerParams(dimension_semantics=("arbitrary",)),
        name="cast_tail",
    )(w3, w3)


def _seg_cumsum(x, axis, reverse):
    n = x.shape[axis]
    pos = lax.broadcasted_iota(jnp.int32, x.shape, axis) % CHUNK
    s = 1
    while s < CHUNK:
        if reverse:
            x = x + jnp.where(pos < CHUNK - s, pltpu.roll(x, n - s, axis), 0.0)
        else:
            x = x + jnp.where(pos >= s, pltpu.roll(x, s, axis), 0.0)
        s *= 2
    return x


def _decay_and_beta(ab, alog, dtb, axis, nh):
    pre = ab + dtb
    softplus = jnp.maximum(pre, 0.0) + jnp.log(1.0 + jnp.exp(-jnp.abs(pre)))
    g = -jnp.exp(alog) * softplus * LOG2_E
    beta = jax.nn.sigmoid(ab)
    col = lax.broadcasted_iota(jnp.int32, ab.shape, 1 - axis)
    gf = _seg_cumsum(g, axis, False)
    gb = _seg_cumsum(g, axis, True)
    return jnp.where(col < nh, gf, jnp.where(col < N_DIR * nh, gb, beta))


def _inproj_kernel(x_ref, sc_ref, sh_ref, nw_ref, w_ref, wg_ref, wab_ref, wabt, alog, dtb, alogt, dtbt,
                   p_ref, gz_ref, qkv_ref, zs_ref, sga_ref, sgb_ref, gcs_ref, gcst_ref, *, nh, cw, offs):
    x = x_ref[...]
    h = x * lax.rsqrt(jnp.mean(x * x, axis=-1, keepdims=True) + NORM_EPS) * nw_ref[...]
    h = h * (1.0 + sc_ref[0]) + sh_ref[0]
    hb = h.astype(BF16)

    def proj(group, c0, width):
        which, start = offs[group]
        ref = (w_ref, wg_ref, wab_ref)[which]
        return _dot(hb, ref[:, start + c0:start + c0 + width])

    for c0 in range(0, p_ref.shape[1], cw):
        sl = slice(c0, c0 + cw)
        p_ref[:, sl] = (proj("bg", c0, cw) * proj("ax", c0, cw)).astype(BF16)
        gz_ref[:, sl] = (proj("cg", c0, cw) * _silu(proj("az", c0, cw))).astype(BF16)
    for c0 in range(0, qkv_ref.shape[1], cw):
        qkv_ref[:, c0:c0 + cw] = proj("qkv", c0, cw).astype(BF16)
    for c0 in range(0, zs_ref.shape[1], cw):
        zs_ref[:, c0:c0 + cw] = _silu(proj("z", c0, cw)).astype(BF16)
    for c0 in range(0, sga_ref.shape[1], cw):
        sga_ref[:, c0:c0 + cw] = jax.nn.sigmoid(proj("ga", c0, cw)).astype(BF16)
        sgb_ref[:, c0:c0 + cw] = jax.nn.sigmoid(proj("gb", c0, cw)).astype(BF16)

    gcs_ref[...] = _decay_and_beta(proj("ab", 0, gcs_ref.shape[1]), alog[...], dtb[...], 0, nh)
    gcst_ref[...] = _decay_and_beta(_dot_nt(wabt[...], hb), alogt[...], dtbt[...], 1, nh)


def _inproj(x2, scale, shift, norm_w, w_main, w_gates, w_ab, offs, widths, alog, dtb, *, seq, nh):
    rows, d = x2.shape
    tm = INPROJ_ROWS
    assert seq % tm == 0 and tm % CHUNK == 0
    tiles_per_seq = seq // tm
    nab = widths["ab"]
    resident = lambda a: pl.BlockSpec(a.shape, lambda i: (0,) * a.ndim, pipeline_mode=pl.Buffered(1))
    consts = (norm_w, w_main, w_gates, w_ab, w_ab.T, alog, dtb, alog.T, dtb.T)
    row_blk = lambda w: pl.BlockSpec((tm, w), lambda i: (i, 0))
    mod_blk = pl.BlockSpec((1, 1, d), lambda i: (i // tiles_per_seq, 0, 0))
    out_shapes = [jax.ShapeDtypeStruct((rows, widths[g]), BF16) for g in ("bg", "cg", "qkv", "z", "ga", "gb")]
    out_shapes += [jax.ShapeDtypeStruct((rows, nab), F32), jax.ShapeDtypeStruct((nab, rows), F32)]
    out_specs = [row_blk(s.shape[1]) for s in out_shapes[:-1]]
    out_specs += [pl.BlockSpec((nab, tm), lambda i: (0, i))]
    return pl.pallas_call(
        functools.partial(_inproj_kernel, nh=nh, cw=INPROJ_COLS, offs=offs),
        grid=(rows // tm,),
        in_specs=[row_blk(d), mod_blk, mod_blk] + [resident(a) for a in consts],
        out_specs=out_specs,
        out_shape=out_shapes,
        compiler_params=pltpu.CompilerParams(dimension_semantics=("arbitrary",),
                                             vmem_limit_bytes=VMEM_LIMIT_BYTES),
        name="inproj",
    )(x2, scale, shift, *consts)


def _conv_rows(ext_ref, xm, prev, nxt, w_ref, width):
    r = xm.shape[0]
    half = width // 2
    pad = F32_SUBLANES
    ext_ref[pad:pad + r, :] = xm
    ext_ref[pad - half:pad, :] = prev[prev.shape[0] - half:, :]
    ext_ref[pad + r:pad + r + half, :] = nxt[:half, :]
    acc = None
    for t in range(width):
        term = ext_ref[pad - half + t:pad - half + t + r, :] * w_ref[t:t + 1, :]
        acc = term if acc is None else acc + term
    return acc


def _halo_specs(tile_rows, width_blk, col_of, row_tile_of, total_rows):
    per = tile_rows // BF16_SUBLANES
    last = total_rows // BF16_SUBLANES - 1

    def prev_map(*g):
        return (jnp.maximum(row_tile_of(*g) * per - 1, 0), col_of(*g))

    def next_map(*g):
        return (jnp.minimum((row_tile_of(*g) + 1) * per, last), col_of(*g))

    return (pl.BlockSpec((BF16_SUBLANES, width_blk), prev_map),
            pl.BlockSpec((BF16_SUBLANES, width_blk), next_map))


GROUP = 128


def _half_rows(a, b, second):
    start = b if second else 0
    return jnp.concatenate([a[k + start:k + start + b] for k in range(0, a.shape[0], 2 * b)], axis=0)


def _spread_rows(c, b, second, other):
    parts = []
    for k in range(0, c.shape[0], b):
        pair = (other[k:k + b], c[k:k + b])
        parts.extend(pair if second else pair[::-1])
    return jnp.concatenate(parts, axis=0)


def _zero_row_after(v):
    bits = pltpu.bitcast(v[:F32_SUBLANES, :HEAD_DIM], jnp.int32)
    zero = lax.shift_right_logical(lax.shift_right_logical(bits, 16), 16)
    return zero[:1].astype(F32)


def _block_inverses(pmats, lowers, ii, jj, fillers=()):
    same_block = lambda s: (ii >> s) == (jj >> s)
    eye = jnp.where(ii == jj, 1.0, 0.0)
    fillers = list(fillers)

    def fill(after):
        if fillers:
            fillers.pop(0)(_zero_row_after(after))

    base = F32_SUBLANES.bit_length() - 1
    ps = [jnp.where(same_block(base), p, 0.0) for p in pmats]
    xs = [eye + p for p in ps]
    pbs = [p.astype(BF16) for p in ps]
    p2s = [_dot(pb, pb) for pb in pbs]
    fill(p2s[-1])
    ys = [_dot(p2.astype(BF16), jnp.concatenate([x, p2], axis=1).astype(BF16)) for x, p2 in zip(xs, p2s)]
    xs = [x + y[:, :GROUP] for x, y in zip(xs, ys)]
    fill(ys[-1])
    xs = [x + _dot(y[:, GROUP:].astype(BF16), x.astype(BF16)) for x, y in zip(xs, ys)]

    zeros = jnp.zeros((GROUP // 2, GROUP), F32)
    for s in range(base, CHUNK.bit_length() - 1):
        b = 1 << s
        off = same_block(s + 1) & jnp.logical_not(same_block(s))
        offs = [jnp.where(off, p, 0.0) for p in pmats]
        xbs = [x.astype(BF16) for x in xs]
        ys = [_dot(_half_rows(o, b, lo).astype(BF16), xb) for o, xb, lo in zip(offs, xbs, lowers)]
        ups = [_dot(_half_rows(x, b, lo).astype(BF16), _spread_rows(y, b, lo, zeros).astype(BF16))
               for x, y, lo in zip(xs, ys, lowers)]
        xs = [_spread_rows(_half_rows(x, b, lo) + up, b, lo, _half_rows(x, b, not lo))
              for x, up, lo in zip(xs, ups, lowers)]
        fill(ys[-1])
    assert not fillers
    return xs


def _prep_kernel(qm, qp, qn, km, kp, kn, vm, vp, vn, cwq, cwk, cwv, gcs_ref, gcst_ref,
                 lhs1_ref, lhs2_ref, u_ref, gl_ref, ext_ref, qkv_s, col_s, row_s, *, nrb, nh):
    i = pl.program_id(0)
    j = jnp.minimum(i, pl.num_programs(0) - 2)
    r = (j // nh) % nrb
    h = j % nh
    first = r == 0
    last = r == nrb - 1

    @pl.when(i == 0)
    def _():
        qkv_s[...] = jnp.zeros_like(qkv_s)
        col_s[...] = jnp.zeros_like(col_s)
        row_s[...] = jnp.zeros_like(row_s)

    q, k, v = qkv_s[0], qkv_s[1], qkv_s[2]
    g_col = (col_s[0], col_s[1])
    b_col = (col_s[2], col_s[3])
    nbc = (-b_col[0], -b_col[1])
    g_row = (row_s[0], row_s[1])

    def conv_silu(idx, m, p, n, cw, zero_row):
        pv = jnp.where(first, 0.0, p[...].astype(F32))
        nx = jnp.where(last, 0.0, n[...].astype(F32))
        w = cw[...] + zero_row
        return _silu(_conv_rows(ext_ref.at[idx], m[...].astype(F32), pv, nx, w, CONV_QKV_WIDTH))

    def stage1_q(zero_row):
        q1 = conv_silu(0, qm, qp, qn, cwq, zero_row)
        qkv_s[0] = q1 * (lax.rsqrt(jnp.sum(q1 * q1, axis=-1, keepdims=True) + L2_EPS) * (HEAD_DIM ** -0.5))

    def stage1_k(zero_row):
        k1 = conv_silu(1, km, kp, kn, cwk, zero_row)
        qkv_s[1] = k1 * lax.rsqrt(jnp.sum(k1 * k1, axis=-1, keepdims=True) + L2_EPS)

    def stage1_v(zero_row):
        qkv_s[2] = conv_silu(2, vm, vp, vn, cwv, zero_row)

    def stage1_gates(zero_row):
        del zero_row
        gcs = gcs_ref[...]
        lane = lax.broadcasted_iota(jnp.int32, gcs.shape, 1)
        for slot in range(4):
            picked = jnp.sum(jnp.where(lane == slot * nh + h, gcs, 0.0), axis=1, keepdims=True)
            col_s[slot] = jnp.broadcast_to(picked, col_s.shape[1:])
        for d in range(N_DIR):
            row_s[d] = gcst_ref[pl.ds(d * nh + h, 1), :]

    ii = lax.broadcasted_iota(jnp.int32, (GROUP, GROUP), 0)
    jj = lax.broadcasted_iota(jnp.int32, (GROUP, GROUP), 1)
    same_chunk = (ii // CHUNK) == (jj // CHUNK)
    cpg = GROUP // CHUNK

    inst = []
    for gi in range(q.shape[0] // GROUP):
        rs = slice(gi * GROUP, (gi + 1) * GROUP)
        kc, qc, vc = k[rs], q[rs], v[rs]
        kq = _dot_nt(jnp.concatenate([kc, qc], axis=0).astype(BF16), kc.astype(BF16))
        kk, qk = kq[:GROUP], kq[GROUP:]
        for d in range(N_DIR):
            gc = g_col[d][rs]
            gr = g_row[d][:, rs]
            bc = b_col[d][rs]
            incl = same_chunk & ((ii >= jj) if d == 0 else (ii <= jj))
            strict = same_chunk & ((ii > jj) if d == 0 else (ii < jj))
            decay = jnp.exp2(gc - gr)
            pmat = jnp.where(strict, (nbc[d][rs] * kk) * decay, 0.0)
            aqk = jnp.where(incl, qk * decay, 0.0)
            eg = jnp.exp2(gc)
            edge = CHUNK - 1 if d == 0 else 0
            g_tot = jnp.concatenate(
                [jnp.broadcast_to(gc[c * CHUNK + edge:c * CHUNK + edge + 1], (CHUNK, HEAD_DIM))
                 for c in range(cpg)], axis=0)
            kdt = (kc * jnp.exp2(g_tot - gc)).T
            qd = qc * eg
            for c in range(cpg):
                cs = slice(c * CHUNK, (c + 1) * CHUNK)
                lhs1_ref[d, gi * cpg + c, CHUNK:, :] = qd[cs].astype(BF16)
                gl_ref[d, gi * cpg + c] = jnp.exp2(g_tot[c * CHUNK:c * CHUNK + 1])
            for p in range(cpg // 2):
                ls = slice(p * 2 * CHUNK, (p + 1) * 2 * CHUNK)
                a_pair = (aqk[p * 2 * CHUNK:p * 2 * CHUNK + CHUNK, ls]
                          + aqk[p * 2 * CHUNK + CHUNK:(p + 1) * 2 * CHUNK, ls])
                lhs2_ref[d, gi * (cpg // 2) + p] = jnp.concatenate([a_pair, kdt[:, ls]], axis=0).astype(BF16)
            rhs = jnp.concatenate([vc * bc, kc * (bc * eg)], axis=1).astype(BF16)
            inst.append(dict(gi=gi, d=d, pmat=pmat, rhs=rhs))

    tmats = _block_inverses([it["pmat"] for it in inst], [it["d"] == 0 for it in inst], ii, jj,
                            fillers=[stage1_gates, stage1_q, stage1_k, stage1_v])

    for it, tmat in zip(inst, tmats):
        gi, d = it["gi"], it["d"]
        uw = _dot(tmat.astype(BF16), it["rhs"])
        for c in range(cpg):
            cs = slice(c * CHUNK, (c + 1) * CHUNK)
            lhs1_ref[d, gi * cpg + c, :CHUNK, :] = uw[cs, HEAD_DIM:].astype(BF16)
            u_ref[d, gi * cpg + c] = uw[cs, :HEAD_DIM].astype(BF16)


def _prep(qkv, conv_w, gcs, gcst, *, bsz, seq, nh):
    rows = qkv.shape[0]
    rr = PREP_ROWS
    nrb = seq // rr
    ncb = rr // CHUNK
    nc = seq // CHUNK
    ngc = gcs.shape[1]
    n_tiles = bsz * nrb * nh
    assert GROUP == HEAD_DIM and rr % GROUP == 0

    tile_in = lambda i: jnp.minimum(i, n_tiles - 1)
    tile_out = lambda i: jnp.maximum(i - 1, 0)
    row_of = lambda i: tile_in(i) // nh
    head_of = lambda i: tile_in(i) % nh
    in_specs = []
    for part in range(3):
        col_of = functools.partial(lambda i, part: part * nh + head_of(i), part=part)
        in_specs.append(pl.BlockSpec((rr, HEAD_DIM), functools.partial(
            lambda i, col_of: (row_of(i), col_of(i)), col_of=col_of)))
        in_specs.extend(_halo_specs(rr, HEAD_DIM, col_of, row_of, rows))
    for part in range(3):
        in_specs.append(pl.BlockSpec((CONV_QKV_WIDTH, HEAD_DIM), functools.partial(
            lambda i, part: (0, part * nh + head_of(i)), part=part)))
    in_specs.append(pl.BlockSpec((rr, ngc), lambda i: (row_of(i), 0)))
    in_specs.append(pl.BlockSpec((ngc, rr), lambda i: (0, row_of(i))))

    def out(per_block, shape_tail, dtype):
        blk = (None, N_DIR, None, per_block) + shape_tail
        zeros = (0,) * len(shape_tail)

        def imap(i):
            t = tile_out(i)
            return (t // (nrb * nh), 0, t % nh, (t // nh) % nrb) + zeros

        return (jax.ShapeDtypeStruct((bsz, N_DIR, nh, per_block * nrb) + shape_tail, dtype),
                pl.BlockSpec(blk, imap))

    outs = [out(ncb, (2 * CHUNK, HEAD_DIM), BF16), out(ncb // 2, (CHUNK + HEAD_DIM, 2 * CHUNK), BF16),
            out(ncb, (CHUNK, HEAD_DIM), BF16), out(ncb, (1, HEAD_DIM), F32)]
    return pl.pallas_call(
        functools.partial(_prep_kernel, nrb=nrb, nh=nh),
        grid=(n_tiles + 1,),
        in_specs=in_specs,
        out_specs=[o[1] for o in outs],
        out_shape=[o[0] for o in outs],
        scratch_shapes=[pltpu.VMEM((3, rr + 2 * F32_SUBLANES, HEAD_DIM), F32),
                        pltpu.VMEM((3, rr, HEAD_DIM), F32),
                        pltpu.VMEM((4, rr, HEAD_DIM), F32),
                        pltpu.VMEM((N_DIR, 1, rr), F32)],
        compiler_params=pltpu.CompilerParams(dimension_semantics=("arbitrary",),
                                             vmem_limit_bytes=VMEM_LIMIT_BYTES),
        name="prep",
    )(qkv, qkv, qkv, qkv, qkv, qkv, qkv, qkv, qkv, conv_w, conv_w, conv_w, gcs, gcst)


def _scan_kernel(l1f, l2f, uf, glf, l1b, l2b, ub, glb, of_ref, ob_ref, s_ref, *, ncb, nh):
    @pl.when(pl.program_id(1) == 0)
    def _():
        s_ref[...] = jnp.zeros_like(s_ref)

    dirs = ((l1f, l2f, uf, glf, of_ref), (l1b, l2b, ub, glb, ob_ref))

    zero_rows = jnp.zeros((CHUNK, HEAD_DIM), BF16)

    def chunk_step(pair, parity):
        chains = [(d, h, pair[d], 2 * pair[d] + parity[d]) for d in range(N_DIR) for h in range(nh)]
        states = [s_ref[d, h] for d, h, _, _ in chains]
        r1s = [_dot(dirs[d][0][h, cc], s.astype(BF16))
               for (d, h, _, cc), s in zip(chains, states)]
        v_news = [dirs[d][2][h, cc].astype(F32) - r1[:CHUNK] for (d, h, _, cc), r1 in zip(chains, r1s)]
        r2s = []
        for (d, h, pc, _), v in zip(chains, v_news):
            vb = v.astype(BF16)
            rhs = jnp.concatenate([zero_rows, vb] if parity[d] else [vb, zero_rows], axis=0)
            r2s.append(_dot(dirs[d][1][h, pc], rhs))
        for (d, h, _, cc), s, r1, r2 in zip(chains, states, r1s, r2s):
            s_ref[d, h] = s * dirs[d][3][h, cc] + r2[CHUNK:]
            r0 = pl.multiple_of(cc * CHUNK, CHUNK)
            dirs[d][4][pl.ds(r0, CHUNK), h * HEAD_DIM:(h + 1) * HEAD_DIM] = (
                r1[CHUNK:] + r2[:CHUNK]).astype(dirs[d][4].dtype)

    def body(pi, carry):
        pair = (pi, ncb // 2 - 1 - pi)
        chunk_step(pair, (0, 1))
        chunk_step(pair, (1, 0))
        return carry

    lax.fori_loop(0, ncb // 2, body, 0)


def _scan(lhs1, lhs2, u, gl, *, seq):
    bsz, _, nh, nc = lhs1.shape[:4]
    rr = SCAN_ROWS
    ncb = rr // CHUNK
    nt = nc // ncb

    def specs(d):
        def mk(arr):
            tail = arr.shape[4:]
            zeros = (0,) * len(tail)
            if d == 0:
                imap = lambda b, t: (b, 0, 0, t) + zeros
            else:
                imap = lambda b, t: (b, 1, 0, nt - 1 - t) + zeros
            return pl.BlockSpec((None, None, nh, arr.shape[3] // nt) + tail, imap)
        return [mk(a) for a in (lhs1, lhs2, u, gl)]

    o_shape = jax.ShapeDtypeStruct((bsz * seq, nh * HEAD_DIM), BF16)
    return pl.pallas_call(
        functools.partial(_scan_kernel, ncb=ncb, nh=nh),
        grid=(bsz, nt),
        in_specs=specs(0) + specs(1),
        out_specs=[pl.BlockSpec((rr, nh * HEAD_DIM), lambda b, t: (b * nt + t, 0)),
                   pl.BlockSpec((rr, nh * HEAD_DIM), lambda b, t: (b * nt + nt - 1 - t, 0))],
        out_shape=[o_shape, o_shape],
        scratch_shapes=[pltpu.VMEM((N_DIR, nh, HEAD_DIM, HEAD_DIM), F32)],
        compiler_params=pltpu.CompilerParams(dimension_semantics=("arbitrary", "arbitrary"),
                                             vmem_limit_bytes=VMEM_LIMIT_BYTES),
        name="scan",
    )(lhs1, lhs2, u, gl, lhs1, lhs2, u, gl)


def _out_kernel(x_ref, of_ref, ob_ref, zs_ref, pm, pp, pn, gz_ref, sga_ref, sgb_ref, gate_ref,
                cwa_ref, gnw_ref, fnw_ref, wpa, wpb, wo, y_ref, ext_ref, *, tiles_per_seq, nh, final):
    r = pl.program_id(0) % tiles_per_seq
    sub = x_ref.shape[0] // OUT_SUBTILES
    rows = [slice(i * sub, (i + 1) * sub) for i in range(OUT_SUBTILES)]

    pv = jnp.where(r == 0, 0.0, pp[...].astype(F32))
    nx = jnp.where(r == tiles_per_seq - 1, 0.0, pn[...].astype(F32))
    conv = _conv_rows(ext_ref, pm[...].astype(F32), pv, nx, cwa_ref, CONV_A_WIDTH)

    def yb_input(rs):
        o = of_ref[rs, :].astype(F32) + ob_ref[rs, :].astype(F32)
        heads = []
        for h in range(nh):
            oh = o[:, h * HEAD_DIM:(h + 1) * HEAD_DIM]
            heads.append(oh * lax.rsqrt(jnp.mean(oh * oh, axis=-1, keepdims=True) + NORM_EPS) * gnw_ref[...])
        return (jnp.concatenate(heads, axis=1) * zs_ref[rs, :].astype(F32)).astype(BF16)

    yb_ins = [yb_input(rs) for rs in rows]
    ya_ins = [(gz_ref[rs, :].astype(F32) * conv[rs]).astype(BF16) for rs in rows]
    ybs = [_dot(v, wpb[...]) for v in yb_ins]
    yas = [_dot(v, wpa[...]) for v in ya_ins]
    mergeds = [(sga_ref[rs, :].astype(F32) * ya + sgb_ref[rs, :].astype(F32) * yb).astype(BF16)
               for rs, ya, yb in zip(rows, yas, ybs)]
    deltas = [_dot(m, wo[...]) for m in mergeds]
    for rs, delta in zip(rows, deltas):
        xn = x_ref[rs, :] + gate_ref[0] * delta
        if final:
            xn = xn * lax.rsqrt(jnp.mean(xn * xn, axis=-1, keepdims=True) + NORM_EPS) * fnw_ref[...]
        y_ref[rs, :] = xn


def _out(x2, o_f, o_b, zs, p, gz, sga, sgb, gate, conv_a_w, gdn_norm_w, final_norm_w, wpa, wpb, wo,
         *, seq, nh, final):
    rows, d = x2.shape
    tm = OUT_ROWS
    tiles_per_seq = seq // tm
    row_blk = lambda w: pl.BlockSpec((tm, w), lambda i: (i, 0))
    const = lambda a: pl.BlockSpec(a.shape, lambda i: (0,) * a.ndim)
    halo_prev, halo_next = _halo_specs(tm, p.shape[1], lambda i: 0, lambda i: i, rows)
    return pl.pallas_call(
        functools.partial(_out_kernel, tiles_per_seq=tiles_per_seq, nh=nh, final=final),
        grid=(rows // tm,),
        in_specs=[row_blk(d), row_blk(o_f.shape[1]), row_blk(o_b.shape[1]), row_blk(zs.shape[1]),
                  row_blk(p.shape[1]), halo_prev, halo_next, row_blk(gz.shape[1]),
                  row_blk(sga.shape[1]), row_blk(sgb.shape[1]),
                  pl.BlockSpec((1, 1, d), lambda i: (i // tiles_per_seq, 0, 0)),
                  const(conv_a_w), const(gdn_norm_w), const(final_norm_w),
                  const(wpa), const(wpb), const(wo)],
        out_specs=row_blk(d),
        out_shape=jax.ShapeDtypeStruct((rows, d), F32),
        scratch_shapes=[pltpu.VMEM((tm + 2 * F32_SUBLANES, p.shape[1]), F32)],
        compiler_params=pltpu.CompilerParams(dimension_semantics=("arbitrary",),
                                             vmem_limit_bytes=VMEM_LIMIT_BYTES),
        name="out",
    )(x2, o_f, o_b, zs, p, p, p, gz, sga, sgb, gate, conv_a_w, gdn_norm_w, final_norm_w, wpa, wpb, wo)


def kernel(x, c, w_ada, b_ada, norm_w, w_in, conv_a_w, conv_qkv_w, a_log, dt_bias, gdn_norm_w,
           w_pa, w_pb, w_o, final_norm_w):
    bsz, seq, d = x.shape
    depth = w_ada.shape[0]
    nh = a_log.shape[2]
    d_conv = conv_a_w.shape[2]
    key_dim = nh * HEAD_DIM
    assert conv_qkv_w.shape[2] == 3 * key_dim and gdn_norm_w.shape[1] == HEAD_DIM
    assert seq % PREP_ROWS == 0 and seq % SCAN_ROWS == 0 and seq % OUT_ROWS == 0
    n_gate = N_DIR * nh
    widths = dict(bg=d_conv, cg=d_conv, ax=d_conv, az=d_conv, qkv=3 * key_dim, z=key_dim,
                  ab=2 * n_gate, ga=d, gb=d)
    assert sum(widths.values()) == w_in.shape[2]
    offs, start = {}, 0
    for g, wd in widths.items():
        offs[g] = (0, start)
        start += wd
    ab_start, gates_start = offs["ab"][1], offs["ga"][1]
    offs["ab"], offs["ga"], offs["gb"] = (2, 0), (1, 0), (1, widths["ga"])
    assert all(o % HEAD_DIM == 0 for _, o in offs.values()) and ab_start % HEAD_DIM == 0

    x2 = x.reshape(bsz * seq, d)
    c_pad = jnp.zeros((8, d), F32).at[:bsz].set(c)
    for l in range(depth):
        mod = _ada(c_pad, w_ada[l], b_ada[l][None, :])[:bsz]
        shift, scale, gate = (mod[:, i * d:(i + 1) * d][:, None, :] for i in range(3))

        w_main = _cast_leading_cols(w_in, l, ab_start)
        w_ab, w_gates = _cast_tail_cols(w_in, l, ab_start, gates_start - ab_start, widths["ga"] + widths["gb"])
        zeros = jnp.zeros((1, n_gate), F32)
        alog = jnp.concatenate([a_log[l].reshape(1, n_gate), zeros], axis=1)
        dtb = jnp.concatenate([dt_bias[l].reshape(1, n_gate), zeros], axis=1)
        p, gz, qkv, zs, sga, sgb, gcs, gcst = _inproj(
            x2, scale, shift, norm_w[l][None, :], w_main, w_gates, w_ab, offs, widths, alog, dtb,
            seq=seq, nh=nh)

        lhs1, lhs2, u, gl = _prep(qkv, conv_qkv_w[l], gcs, gcst, bsz=bsz, seq=seq, nh=nh)
        o_f, o_b = _scan(lhs1, lhs2, u, gl, seq=seq)

        x2 = _out(x2, o_f, o_b, zs, p, gz, sga, sgb, gate, conv_a_w[l], gdn_norm_w[l][None, :],
                  final_norm_w[None, :], w_pa[l].astype(BF16), w_pb[l].astype(BF16), w_o[l].astype(BF16),
                  seq=seq, nh=nh, final=(l == depth - 1))
    return x2.reshape(bsz, seq, d)
```

```python
import functools

import jax
import jax.numpy as jnp
from jax import lax
from jax.experimental import pallas as pl
from jax.experimental.pallas import tpu as pltpu

F32 = jnp.float32
BF16 = jnp.bfloat16

HEAD_DIM = 128
CHUNK = 64
N_DIR = 2
CONV_A_WIDTH = 3
CONV_QKV_WIDTH = 5
NORM_EPS = 1e-6
L2_EPS = 1e-6
LOG2_E = 1.4426950408889634
BF16_SUBLANES = 16
F32_SUBLANES = 8
VMEM_LIMIT_BYTES = 56 * 1024 * 1024

INPROJ_ROWS = 512
INPROJ_COLS = 512
PREP_ROWS = 1024
SCAN_ROWS = 512
OUT_ROWS = 512
OUT_SUBTILES = 2


def _dot(a, b):
    return jnp.dot(a, b, preferred_element_type=F32)


def _dot_nt(a, b):
    return lax.dot_general(a, b, (((1,), (1,)), ((), ())), preferred_element_type=F32)


def _silu(x):
    return x * jax.nn.sigmoid(x)


def _ada_kernel(c_ref, w_ref, b_ref, o_ref):
    sc = _silu(c_ref[...])
    o_ref[...] = jnp.dot(sc, w_ref[...], preferred_element_type=F32,
                         precision=lax.Precision.HIGHEST) + b_ref[...]


def _ada(c_pad, w, b):
    m, d = c_pad.shape
    n = w.shape[1]
    bn = d
    return pl.pallas_call(
        _ada_kernel,
        grid=(n // bn,),
        in_specs=[pl.BlockSpec((m, d), lambda j: (0, 0)),
                  pl.BlockSpec((d, bn), lambda j: (0, j)),
                  pl.BlockSpec((1, bn), lambda j: (0, j))],
        out_specs=pl.BlockSpec((m, bn), lambda j: (0, j)),
        out_shape=jax.ShapeDtypeStruct((m, n), F32),
        name="ada",
    )(c_pad, w, b)


def _seg_cumsum(x, axis, reverse):
    n = x.shape[axis]
    pos = lax.broadcasted_iota(jnp.int32, x.shape, axis) % CHUNK
    s = 1
    while s < CHUNK:
        if reverse:
            x = x + jnp.where(pos < CHUNK - s, pltpu.roll(x, n - s, axis), 0.0)
        else:
            x = x + jnp.where(pos >= s, pltpu.roll(x, s, axis), 0.0)
        s *= 2
    return x


def _decay_and_beta(ab, alog, dtb, axis, nh):
    pre = ab + dtb
    softplus = jnp.maximum(pre, 0.0) + jnp.log(1.0 + jnp.exp(-jnp.abs(pre)))
    g = -jnp.exp(alog) * softplus * LOG2_E
    beta = jax.nn.sigmoid(ab)
    col = lax.broadcasted_iota(jnp.int32, ab.shape, 1 - axis)
    gf = _seg_cumsum(g, axis, False)
    gb = _seg_cumsum(g, axis, True)
    return jnp.where(col < nh, gf, jnp.where(col < N_DIR * nh, gb, beta))


def _inproj_kernel(x_ref, sc_ref, sh_ref, nw_ref, wt_ref, alog, dtb, alogt, dtbt,
                   p_ref, gz_ref, qkv_ref, zs_ref, sga_ref, sgb_ref, gcs_ref, gcst_ref, *, nh, cw, offs):
    x = x_ref[...]
    h = x * lax.rsqrt(jnp.mean(x * x, axis=-1, keepdims=True) + NORM_EPS) * nw_ref[...]
    h = h * (1.0 + sc_ref[0]) + sh_ref[0]
    hb = h.astype(BF16)

    def proj(group, c0, width):
        start = offs[group] + c0
        return _dot_nt(hb, wt_ref[start:start + width, :])

    for c0 in range(0, p_ref.shape[1], cw):
        sl = slice(c0, c0 + cw)
        p_ref[:, sl] = (proj("bg", c0, cw) * proj("ax", c0, cw)).astype(BF16)
        gz_ref[:, sl] = (proj("cg", c0, cw) * _silu(proj("az", c0, cw))).astype(BF16)
    for c0 in range(0, qkv_ref.shape[1], cw):
        qkv_ref[:, c0:c0 + cw] = proj("qkv", c0, cw).astype(BF16)
    for c0 in range(0, zs_ref.shape[1], cw):
        zs_ref[:, c0:c0 + cw] = _silu(proj("z", c0, cw)).astype(BF16)
    for c0 in range(0, sga_ref.shape[1], cw):
        sga_ref[:, c0:c0 + cw] = jax.nn.sigmoid(proj("ga", c0, cw)).astype(BF16)
        sgb_ref[:, c0:c0 + cw] = jax.nn.sigmoid(proj("gb", c0, cw)).astype(BF16)

    nab = gcs_ref.shape[1]
    gcs_ref[...] = _decay_and_beta(proj("ab", 0, nab), alog[...], dtb[...], 0, nh)
    wabt = wt_ref[offs["ab"]:offs["ab"] + nab, :]
    gcst_ref[...] = _decay_and_beta(_dot_nt(wabt, hb), alogt[...], dtbt[...], 1, nh)


def _inproj(x2, scale, shift, norm_w, wt, offs, widths, alog, dtb, *, seq, nh):
    rows, d = x2.shape
    tm = INPROJ_ROWS
    assert seq % tm == 0 and tm % CHUNK == 0
    tiles_per_seq = seq // tm
    nab = widths["ab"]
    resident = lambda a: pl.BlockSpec(a.shape, lambda i: (0,) * a.ndim, pipeline_mode=pl.Buffered(1))
    consts = (norm_w, wt, alog, dtb, alog.T, dtb.T)
    row_blk = lambda w: pl.BlockSpec((tm, w), lambda i: (i, 0))
    mod_blk = pl.BlockSpec((1, 1, d), lambda i: (i // tiles_per_seq, 0, 0))
    out_shapes = [jax.ShapeDtypeStruct((rows, widths[g]), BF16) for g in ("bg", "cg", "qkv", "z", "ga", "gb")]
    out_shapes += [jax.ShapeDtypeStruct((rows, nab), F32), jax.ShapeDtypeStruct((nab, rows), F32)]
    out_specs = [row_blk(s.shape[1]) for s in out_shapes[:-1]]
    out_specs += [pl.BlockSpec((nab, tm), lambda i: (0, i))]
    return pl.pallas_call(
        functools.partial(_inproj_kernel, nh=nh, cw=INPROJ_COLS, offs=offs),
        grid=(rows // tm,),
        in_specs=[row_blk(d), mod_blk, mod_blk] + [resident(a) for a in consts],
        out_specs=out_specs,
        out_shape=out_shapes,
        compiler_params=pltpu.CompilerParams(dimension_semantics=("arbitrary",),
                                             vmem_limit_bytes=VMEM_LIMIT_BYTES),
        name="inproj",
    )(x2, scale, shift, *consts)


def _conv_rows(ext_ref, xm, prev, nxt, w_ref, width):
    r = xm.shape[0]
    half = width // 2
    pad = F32_SUBLANES
    ext_ref[pad:pad + r, :] = xm
    ext_ref[pad - half:pad, :] = prev[prev.shape[0] - half:, :]
    ext_ref[pad + r:pad + r + half, :] = nxt[:half, :]
    acc = None
    for t in range(width):
        term = ext_ref[pad - half + t:pad - half + t + r, :] * w_ref[t:t + 1, :]
        acc = term if acc is None else acc + term
    return acc


def _halo_specs(tile_rows, width_blk, col_of, row_tile_of, total_rows):
    per = tile_rows // BF16_SUBLANES
    last = total_rows // BF16_SUBLANES - 1

    def prev_map(*g):
        return (jnp.maximum(row_tile_of(*g) * per - 1, 0), col_of(*g))

    def next_map(*g):
        return (jnp.minimum((row_tile_of(*g) + 1) * per, last), col_of(*g))

    return (pl.BlockSpec((BF16_SUBLANES, width_blk), prev_map),
            pl.BlockSpec((BF16_SUBLANES, width_blk), next_map))


GROUP = 128


def _half_rows(a, b, second):
    start = b if second else 0
    return jnp.concatenate([a[k + start:k + start + b] for k in range(0, a.shape[0], 2 * b)], axis=0)


def _spread_rows(c, b, second, other):
    parts = []
    for k in range(0, c.shape[0], b):
        pair = (other[k:k + b], c[k:k + b])
        parts.extend(pair if second else pair[::-1])
    return jnp.concatenate(parts, axis=0)


def _zero_row_after(v):
    bits = pltpu.bitcast(v[:F32_SUBLANES, :HEAD_DIM], jnp.int32)
    zero = lax.shift_right_logical(lax.shift_right_logical(bits, 16), 16)
    return zero[:1].astype(F32)


def _block_inverses(pmats, lowers, ii, jj, fillers=()):
    same_block = lambda s: (ii >> s) == (jj >> s)
    eye = jnp.where(ii == jj, 1.0, 0.0)
    fillers = list(fillers)

    def fill(after):
        if fillers:
            fillers.pop(0)(_zero_row_after(after))

    base = F32_SUBLANES.bit_length() - 1
    ps = [jnp.where(same_block(base), p, 0.0) for p in pmats]
    xs = [eye + p for p in ps]
    pbs = [p.astype(BF16) for p in ps]
    p2s = [_dot(pb, pb) for pb in pbs]
    fill(p2s[-1])
    ys = [_dot(p2.astype(BF16), jnp.concatenate([x, p2], axis=1).astype(BF16)) for x, p2 in zip(xs, p2s)]
    xs = [x + y[:, :GROUP] for x, y in zip(xs, ys)]
    fill(ys[-1])
    xs = [x + _dot(y[:, GROUP:].astype(BF16), x.astype(BF16)) for x, y in zip(xs, ys)]

    zeros = jnp.zeros((GROUP // 2, GROUP), F32)
    for s in range(base, CHUNK.bit_length() - 1):
        b = 1 << s
        off = same_block(s + 1) & jnp.logical_not(same_block(s))
        offs = [jnp.where(off, p, 0.0) for p in pmats]
        xbs = [x.astype(BF16) for x in xs]
        ys = [_dot(_half_rows(o, b, lo).astype(BF16), xb) for o, xb, lo in zip(offs, xbs, lowers)]
        ups = [_dot(_half_rows(x, b, lo).astype(BF16), _spread_rows(y, b, lo, zeros).astype(BF16))
               for x, y, lo in zip(xs, ys, lowers)]
        xs = [_spread_rows(_half_rows(x, b, lo) + up, b, lo, _half_rows(x, b, not lo))
              for x, up, lo in zip(xs, ups, lowers)]
        fill(ys[-1])
    assert not fillers
    return xs


def _prep_kernel(qm, qp, qn, km, kp, kn, vm, vp, vn, cwq, cwk, cwv, gcs_ref, gcst_ref,
                 lhs1_ref, lhs2_ref, u_ref, gl_ref, ext_ref, qkv_s, col_s, row_s, *, nrb, nh):
    i = pl.program_id(0)
    j = jnp.minimum(i, pl.num_programs(0) - 2)
    r = (j // nh) % nrb
    h = j % nh
    first = r == 0
    last = r == nrb - 1

    @pl.when(i == 0)
    def _():
        qkv_s[...] = jnp.zeros_like(qkv_s)
        col_s[...] = jnp.zeros_like(col_s)
        row_s[...] = jnp.zeros_like(row_s)

    q, k, v = qkv_s[0], qkv_s[1], qkv_s[2]
    g_col = (col_s[0], col_s[1])
    b_col = (col_s[2], col_s[3])
    nbc = (-b_col[0], -b_col[1])
    g_row = (row_s[0], row_s[1])

    def conv_silu(idx, m, p, n, cw, zero_row):
        pv = jnp.where(first, 0.0, p[...].astype(F32))
        nx = jnp.where(last, 0.0, n[...].astype(F32))
        w = cw[...] + zero_row
        return _silu(_conv_rows(ext_ref.at[idx], m[...].astype(F32), pv, nx, w, CONV_QKV_WIDTH))

    def stage1_q(zero_row):
        q1 = conv_silu(0, qm, qp, qn, cwq, zero_row)
        qkv_s[0] = q1 * (lax.rsqrt(jnp.sum(q1 * q1, axis=-1, keepdims=True) + L2_EPS) * (HEAD_DIM ** -0.5))

    def stage1_k(zero_row):
        k1 = conv_silu(1, km, kp, kn, cwk, zero_row)
        qkv_s[1] = k1 * lax.rsqrt(jnp.sum(k1 * k1, axis=-1, keepdims=True) + L2_EPS)

    def stage1_v(zero_row):
        qkv_s[2] = conv_silu(2, vm, vp, vn, cwv, zero_row)

    def stage1_gates(zero_row):
        del zero_row
        gcs = gcs_ref[...]
        lane = lax.broadcasted_iota(jnp.int32, gcs.shape, 1)
        for slot in range(4):
            picked = jnp.sum(jnp.where(lane == slot * nh + h, gcs, 0.0), axis=1, keepdims=True)
            col_s[slot] = jnp.broadcast_to(picked, col_s.shape[1:])
        for d in range(N_DIR):
            row_s[d] = gcst_ref[pl.ds(d * nh + h, 1), :]

    ii = lax.broadcasted_iota(jnp.int32, (GROUP, GROUP), 0)
    jj = lax.broadcasted_iota(jnp.int32, (GROUP, GROUP), 1)
    same_chunk = (ii // CHUNK) == (jj // CHUNK)
    cpg = GROUP // CHUNK

    inst = []
    for gi in range(q.shape[0] // GROUP):
        rs = slice(gi * GROUP, (gi + 1) * GROUP)
        kc, qc, vc = k[rs], q[rs], v[rs]
        kq = _dot_nt(jnp.concatenate([kc, qc], axis=0).astype(BF16), kc.astype(BF16))
        kk, qk = kq[:GROUP], kq[GROUP:]
        for d in range(N_DIR):
            gc = g_col[d][rs]
            gr = g_row[d][:, rs]
            bc = b_col[d][rs]
            incl = same_chunk & ((ii >= jj) if d == 0 else (ii <= jj))
            strict = same_chunk & ((ii > jj) if d == 0 else (ii < jj))
            decay = jnp.exp2(gc - gr)
            pmat = jnp.where(strict, (nbc[d][rs] * kk) * decay, 0.0)
            aqk = jnp.where(incl, qk * decay, 0.0)
            eg = jnp.exp2(gc)
            edge = CHUNK - 1 if d == 0 else 0
            g_tot = jnp.concatenate(
                [jnp.broadcast_to(gc[c * CHUNK + edge:c * CHUNK + edge + 1], (CHUNK, HEAD_DIM))
                 for c in range(cpg)], axis=0)
            kdt = (kc * jnp.exp2(g_tot - gc)).T
            qd = qc * eg
            for c in range(cpg):
                cs = slice(c * CHUNK, (c + 1) * CHUNK)
                lhs1_ref[d, gi * cpg + c, CHUNK:, :] = qd[cs].astype(BF16)
                gl_ref[d, gi * cpg + c] = jnp.exp2(g_tot[c * CHUNK:c * CHUNK + 1])
            for p in range(cpg // 2):
                ls = slice(p * 2 * CHUNK, (p + 1) * 2 * CHUNK)
                a_pair = (aqk[p * 2 * CHUNK:p * 2 * CHUNK + CHUNK, ls]
                          + aqk[p * 2 * CHUNK + CHUNK:(p + 1) * 2 * CHUNK, ls])
                lhs2_ref[d, gi * (cpg // 2) + p] = jnp.concatenate([a_pair, kdt[:, ls]], axis=0).astype(BF16)
            rhs = jnp.concatenate([vc * bc, kc * (bc * eg)], axis=1).astype(BF16)
            inst.append(dict(gi=gi, d=d, pmat=pmat, rhs=rhs))

    tmats = _block_inverses([it["pmat"] for it in inst], [it["d"] == 0 for it in inst], ii, jj,
                            fillers=[stage1_gates, stage1_q, stage1_k, stage1_v])

    for it, tmat in zip(inst, tmats):
        gi, d = it["gi"], it["d"]
        uw = _dot(tmat.astype(BF16), it["rhs"])
        for c in range(cpg):
            cs = slice(c * CHUNK, (c + 1) * CHUNK)
            lhs1_ref[d, gi * cpg + c, :CHUNK, :] = uw[cs, HEAD_DIM:].astype(BF16)
            u_ref[d, gi * cpg + c] = uw[cs, :HEAD_DIM].astype(BF16)


def _prep(qkv, conv_w, gcs, gcst, *, bsz, seq, nh):
    rows = qkv.shape[0]
    rr = PREP_ROWS
    nrb = seq // rr
    ncb = rr // CHUNK
    nc = seq // CHUNK
    ngc = gcs.shape[1]
    n_tiles = bsz * nrb * nh
    assert GROUP == HEAD_DIM and rr % GROUP == 0

    tile_in = lambda i: jnp.minimum(i, n_tiles - 1)
    tile_out = lambda i: jnp.maximum(i - 1, 0)
    row_of = lambda i: tile_in(i) // nh
    head_of = lambda i: tile_in(i) % nh
    in_specs = []
    for part in range(3):
        col_of = functools.partial(lambda i, part: part * nh + head_of(i), part=part)
        in_specs.append(pl.BlockSpec((rr, HEAD_DIM), functools.partial(
            lambda i, col_of: (row_of(i), col_of(i)), col_of=col_of)))
        in_specs.extend(_halo_specs(rr, HEAD_DIM, col_of, row_of, rows))
    for part in range(3):
        in_specs.append(pl.BlockSpec((CONV_QKV_WIDTH, HEAD_DIM), functools.partial(
            lambda i, part: (0, part * nh + head_of(i)), part=part)))
    in_specs.append(pl.BlockSpec((rr, ngc), lambda i: (row_of(i), 0)))
    in_specs.append(pl.BlockSpec((ngc, rr), lambda i: (0, row_of(i))))

    def out(per_block, shape_tail, dtype):
        blk = (None, N_DIR, None, per_block) + shape_tail
        zeros = (0,) * len(shape_tail)

        def imap(i):
            t = tile_out(i)
            return (t // (nrb * nh), 0, t % nh, (t // nh) % nrb) + zeros

        return (jax.ShapeDtypeStruct((bsz, N_DIR, nh, per_block * nrb) + shape_tail, dtype),
                pl.BlockSpec(blk, imap))

    outs = [out(ncb, (2 * CHUNK, HEAD_DIM), BF16), out(ncb // 2, (CHUNK + HEAD_DIM, 2 * CHUNK), BF16),
            out(ncb, (CHUNK, HEAD_DIM), BF16), out(ncb, (1, HEAD_DIM), F32)]
    return pl.pallas_call(
        functools.partial(_prep_kernel, nrb=nrb, nh=nh),
        grid=(n_tiles + 1,),
        in_specs=in_specs,
        out_specs=[o[1] for o in outs],
        out_shape=[o[0] for o in outs],
        scratch_shapes=[pltpu.VMEM((3, rr + 2 * F32_SUBLANES, HEAD_DIM), F32),
                        pltpu.VMEM((3, rr, HEAD_DIM), F32),
                        pltpu.VMEM((4, rr, HEAD_DIM), F32),
                        pltpu.VMEM((N_DIR, 1, rr), F32)],
        compiler_params=pltpu.CompilerParams(dimension_semantics=("arbitrary",),
                                             vmem_limit_bytes=VMEM_LIMIT_BYTES),
        name="prep",
    )(qkv, qkv, qkv, qkv, qkv, qkv, qkv, qkv, qkv, conv_w, conv_w, conv_w, gcs, gcst)


def _scan_kernel(l1f, l2f, uf, glf, l1b, l2b, ub, glb, of_ref, ob_ref, s_ref, *, ncb, nh):
    @pl.when(pl.program_id(1) == 0)
    def _():
        s_ref[...] = jnp.zeros_like(s_ref)

    dirs = ((l1f, l2f, uf, glf, of_ref), (l1b, l2b, ub, glb, ob_ref))

    zero_rows = jnp.zeros((CHUNK, HEAD_DIM), BF16)

    def chunk_step(pair, parity):
        chains = [(d, h, pair[d], 2 * pair[d] + parity[d]) for d in range(N_DIR) for h in range(nh)]
        states = [s_ref[d, h] for d, h, _, _ in chains]
        r1s = [_dot(dirs[d][0][h, cc], s.astype(BF16))
               for (d, h, _, cc), s in zip(chains, states)]
        v_news = [dirs[d][2][h, cc].astype(F32) - r1[:CHUNK] for (d, h, _, cc), r1 in zip(chains, r1s)]
        r2s = []
        for (d, h, pc, _), v in zip(chains, v_news):
            vb = v.astype(BF16)
            rhs = jnp.concatenate([zero_rows, vb] if parity[d] else [vb, zero_rows], axis=0)
            r2s.append(_dot(dirs[d][1][h, pc], rhs))
        for (d, h, _, cc), s, r1, r2 in zip(chains, states, r1s, r2s):
            s_ref[d, h] = s * dirs[d][3][h, cc] + r2[CHUNK:]
            r0 = pl.multiple_of(cc * CHUNK, CHUNK)
            dirs[d][4][pl.ds(r0, CHUNK), h * HEAD_DIM:(h + 1) * HEAD_DIM] = (
                r1[CHUNK:] + r2[:CHUNK]).astype(dirs[d][4].dtype)

    def body(pi, carry):
        pair = (pi, ncb // 2 - 1 - pi)
        chunk_step(pair, (0, 1))
        chunk_step(pair, (1, 0))
        return carry

    lax.fori_loop(0, ncb // 2, body, 0)


def _scan(lhs1, lhs2, u, gl, *, seq):
    bsz, _, nh, nc = lhs1.shape[:4]
    rr = SCAN_ROWS
    ncb = rr // CHUNK
    nt = nc // ncb

    def specs(d):
        def mk(arr):
            tail = arr.shape[4:]
            zeros = (0,) * len(tail)
            if d == 0:
                imap = lambda b, t: (b, 0, 0, t) + zeros
            else:
                imap = lambda b, t: (b, 1, 0, nt - 1 - t) + zeros
            return pl.BlockSpec((None, None, nh, arr.shape[3] // nt) + tail, imap)
        return [mk(a) for a in (lhs1, lhs2, u, gl)]

    o_shape = jax.ShapeDtypeStruct((bsz * seq, nh * HEAD_DIM), BF16)
    return pl.pallas_call(
        functools.partial(_scan_kernel, ncb=ncb, nh=nh),
        grid=(bsz, nt),
        in_specs=specs(0) + specs(1),
        out_specs=[pl.BlockSpec((rr, nh * HEAD_DIM), lambda b, t: (b * nt + t, 0)),
                   pl.BlockSpec((rr, nh * HEAD_DIM), lambda b, t: (b * nt + nt - 1 - t, 0))],
        out_shape=[o_shape, o_shape],
        scratch_shapes=[pltpu.VMEM((N_DIR, nh, HEAD_DIM, HEAD_DIM), F32)],
        compiler_params=pltpu.CompilerParams(dimension_semantics=("arbitrary", "arbitrary"),
                                             vmem_limit_bytes=VMEM_LIMIT_BYTES),
        name="scan",
    )(lhs1, lhs2, u, gl, lhs1, lhs2, u, gl)


def _out_kernel(x_ref, of_ref, ob_ref, zs_ref, pm, pp, pn, gz_ref, sga_ref, sgb_ref, gate_ref,
                cwa_ref, gnw_ref, fnw_ref, wpa, wpb, wo, y_ref, ext_ref, *, tiles_per_seq, nh, final):
    r = pl.program_id(0) % tiles_per_seq
    sub = x_ref.shape[0] // OUT_SUBTILES
    rows = [slice(i * sub, (i + 1) * sub) for i in range(OUT_SUBTILES)]

    pv = jnp.where(r == 0, 0.0, pp[...].astype(F32))
    nx = jnp.where(r == tiles_per_seq - 1, 0.0, pn[...].astype(F32))
    conv = _conv_rows(ext_ref, pm[...].astype(F32), pv, nx, cwa_ref, CONV_A_WIDTH)

    def yb_input(rs):
        o = of_ref[rs, :].astype(F32) + ob_ref[rs, :].astype(F32)
        heads = []
        for h in range(nh):
            oh = o[:, h * HEAD_DIM:(h + 1) * HEAD_DIM]
            heads.append(oh * lax.rsqrt(jnp.mean(oh * oh, axis=-1, keepdims=True) + NORM_EPS) * gnw_ref[...])
        return (jnp.concatenate(heads, axis=1) * zs_ref[rs, :].astype(F32)).astype(BF16)

    yb_ins = [yb_input(rs) for rs in rows]
    ya_ins = [(gz_ref[rs, :].astype(F32) * conv[rs]).astype(BF16) for rs in rows]
    ybs = [_dot(v, wpb[...]) for v in yb_ins]
    yas = [_dot(v, wpa[...]) for v in ya_ins]
    mergeds = [(sga_ref[rs, :].astype(F32) * ya + sgb_ref[rs, :].astype(F32) * yb).astype(BF16)
               for rs, ya, yb in zip(rows, yas, ybs)]
    deltas = [_dot(m, wo[...]) for m in mergeds]
    for rs, delta in zip(rows, deltas):
        xn = x_ref[rs, :] + gate_ref[0] * delta
        if final:
            xn = xn * lax.rsqrt(jnp.mean(xn * xn, axis=-1, keepdims=True) + NORM_EPS) * fnw_ref[...]
        y_ref[rs, :] = xn


def _out(x2, o_f, o_b, zs, p, gz, sga, sgb, gate, conv_a_w, gdn_norm_w, final_norm_w, wpa, wpb, wo,
         *, seq, nh, final):
    rows, d = x2.shape
    tm = OUT_ROWS
    tiles_per_seq = seq // tm
    row_blk = lambda w: pl.BlockSpec((tm, w), lambda i: (i, 0))
    const = lambda a: pl.BlockSpec(a.shape, lambda i: (0,) * a.ndim)
    halo_prev, halo_next = _halo_specs(tm, p.shape[1], lambda i: 0, lambda i: i, rows)
    return pl.pallas_call(
        functools.partial(_out_kernel, tiles_per_seq=tiles_per_seq, nh=nh, final=final),
        grid=(rows // tm,),
        in_specs=[row_blk(d), row_blk(o_f.shape[1]), row_blk(o_b.shape[1]), row_blk(zs.shape[1]),
                  row_blk(p.shape[1]), halo_prev, halo_next, row_blk(gz.shape[1]),
                  row_blk(sga.shape[1]), row_blk(sgb.shape[1]),
                  pl.BlockSpec((1, 1, d), lambda i: (i // tiles_per_seq, 0, 0)),
                  const(conv_a_w), const(gdn_norm_w), const(final_norm_w),
                  const(wpa), const(wpb), const(wo)],
        out_specs=row_blk(d),
        out_shape=jax.ShapeDtypeStruct((rows, d), F32),
        scratch_shapes=[pltpu.VMEM((tm + 2 * F32_SUBLANES, p.shape[1]), F32)],
        compiler_params=pltpu.CompilerParams(dimension_semantics=("arbitrary",),
                                             vmem_limit_bytes=VMEM_LIMIT_BYTES),
        name="out",
    )(x2, o_f, o_b, zs, p, p, p, gz, sga, sgb, gate, conv_a_w, gdn_norm_w, final_norm_w, wpa, wpb, wo)


def kernel(x, c, w_ada, b_ada, norm_w, w_in, conv_a_w, conv_qkv_w, a_log, dt_bias, gdn_norm_w,
           w_pa, w_pb, w_o, final_norm_w):
    bsz, seq, d = x.shape
    depth = w_ada.shape[0]
    nh = a_log.shape[2]
    d_conv = conv_a_w.shape[2]
    key_dim = nh * HEAD_DIM
    assert conv_qkv_w.shape[2] == 3 * key_dim and gdn_norm_w.shape[1] == HEAD_DIM
    assert seq % PREP_ROWS == 0 and seq % SCAN_ROWS == 0 and seq % OUT_ROWS == 0
    n_gate = N_DIR * nh
    widths = dict(bg=d_conv, cg=d_conv, ax=d_conv, az=d_conv, qkv=3 * key_dim, z=key_dim,
                  ab=2 * n_gate, ga=d, gb=d)
    assert sum(widths.values()) == w_in.shape[2]
    offs, start = {}, 0
    for g, wd in widths.items():
        offs[g] = start
        start += wd
    assert all(o % BF16_SUBLANES == 0 for o in offs.values())

    x2 = x.reshape(bsz * seq, d)
    c_pad = jnp.zeros((8, d), F32).at[:bsz].set(c)
    for l in range(depth):
        mod = _ada(c_pad, w_ada[l], b_ada[l][None, :])[:bsz]
        shift, scale, gate = (mod[:, i * d:(i + 1) * d][:, None, :] for i in range(3))

        wt = w_in[l].T.astype(BF16)
        zeros = jnp.zeros((1, n_gate), F32)
        alog = jnp.concatenate([a_log[l].reshape(1, n_gate), zeros], axis=1)
        dtb = jnp.concatenate([dt_bias[l].reshape(1, n_gate), zeros], axis=1)
        p, gz, qkv, zs, sga, sgb, gcs, gcst = _inproj(
            x2, scale, shift, norm_w[l][None, :], wt, offs, widths, alog, dtb, seq=seq, nh=nh)

        lhs1, lhs2, u, gl = _prep(qkv, conv_qkv_w[l], gcs, gcst, bsz=bsz, seq=seq, nh=nh)
        o_f, o_b = _scan(lhs1, lhs2, u, gl, seq=seq)

        x2 = _out(x2, o_f, o_b, zs, p, gz, sga, sgb, gate, conv_a_w[l], gdn_norm_w[l][None, :],
                  final_norm_w[None, :], w_pa[l].astype(BF16), w_pb[l].astype(BF16), w_o[l].astype(BF16),
                  seq=seq, nh=nh, final=(l == depth - 1))
    return x2.reshape(bsz, seq, d)
```

```python
import functools

import jax
import jax.numpy as jnp
from jax import lax
from jax.experimental import pallas as pl
from jax.experimental.pallas import tpu as pltpu

F32 = jnp.float32
BF16 = jnp.bfloat16

HEAD_DIM = 128
CHUNK = 64
N_DIR = 2
CONV_A_WIDTH = 3
CONV_QKV_WIDTH = 5
NORM_EPS = 1e-6
L2_EPS = 1e-6
LOG2_E = 1.4426950408889634
BF16_SUBLANES = 16
F32_SUBLANES = 8
VMEM_LIMIT_BYTES = 56 * 1024 * 1024

INPROJ_ROWS = 512
INPROJ_COLS = 512
PREP_ROWS = 1024
SCAN_ROWS = 512
OUT_ROWS = 512
OUT_SUBTILES = 2


def _dot(a, b):
    return jnp.dot(a, b, preferred_element_type=F32)


def _dot_nt(a, b):
    return lax.dot_general(a, b, (((1,), (1,)), ((), ())), preferred_element_type=F32)


def _silu(x):
    return x * jax.nn.sigmoid(x)


def _ada_kernel(c_ref, w_ref, b_ref, o_ref):
    sc = _silu(c_ref[...])
    o_ref[...] = jnp.dot(sc, w_ref[...], preferred_element_type=F32,
                         precision=lax.Precision.HIGHEST) + b_ref[...]


def _ada(c_pad, w, b):
    m, d = c_pad.shape
    n = w.shape[1]
    bn = d
    return pl.pallas_call(
        _ada_kernel,
        grid=(n // bn,),
        in_specs=[pl.BlockSpec((m, d), lambda j: (0, 0)),
                  pl.BlockSpec((d, bn), lambda j: (0, j)),
                  pl.BlockSpec((1, bn), lambda j: (0, j))],
        out_specs=pl.BlockSpec((m, bn), lambda j: (0, j)),
        out_shape=jax.ShapeDtypeStruct((m, n), F32),
        name="ada",
    )(c_pad, w, b)


def _seg_cumsum(x, axis, reverse):
    n = x.shape[axis]
    pos = lax.broadcasted_iota(jnp.int32, x.shape, axis) % CHUNK
    s = 1
    while s < CHUNK:
        if reverse:
            x = x + jnp.where(pos < CHUNK - s, pltpu.roll(x, n - s, axis), 0.0)
        else:
            x = x + jnp.where(pos >= s, pltpu.roll(x, s, axis), 0.0)
        s *= 2
    return x


def _decay_and_beta(ab, alog, dtb, axis, nh):
    pre = ab + dtb
    softplus = jnp.maximum(pre, 0.0) + jnp.log(1.0 + jnp.exp(-jnp.abs(pre)))
    g = -jnp.exp(alog) * softplus * LOG2_E
    beta = jax.nn.sigmoid(ab)
    col = lax.broadcasted_iota(jnp.int32, ab.shape, 1 - axis)
    gf = _seg_cumsum(g, axis, False)
    gb = _seg_cumsum(g, axis, True)
    return jnp.where(col < nh, gf, jnp.where(col < N_DIR * nh, gb, beta))


def _inproj_kernel(x_ref, sc_ref, sh_ref, nw_ref, wt_ref, alogt, dtbt,
                   p_ref, gz_ref, qkv_ref, zs_ref, sga_ref, sgb_ref, gcst_ref, *, nh, cw, offs):
    x = x_ref[...]
    h = x * lax.rsqrt(jnp.mean(x * x, axis=-1, keepdims=True) + NORM_EPS) * nw_ref[...]
    h = h * (1.0 + sc_ref[0]) + sh_ref[0]
    hb = h.astype(BF16)

    def proj(group, c0, width):
        start = offs[group] + c0
        return _dot_nt(hb, wt_ref[start:start + width, :])

    for c0 in range(0, p_ref.shape[1], cw):
        sl = slice(c0, c0 + cw)
        p_ref[:, sl] = (proj("bg", c0, cw) * proj("ax", c0, cw)).astype(BF16)
        gz_ref[:, sl] = (proj("cg", c0, cw) * _silu(proj("az", c0, cw))).astype(BF16)
    for c0 in range(0, qkv_ref.shape[1], cw):
        qkv_ref[:, c0:c0 + cw] = proj("qkv", c0, cw).astype(BF16)
    for c0 in range(0, zs_ref.shape[1], cw):
        zs_ref[:, c0:c0 + cw] = _silu(proj("z", c0, cw)).astype(BF16)
    for c0 in range(0, sga_ref.shape[1], cw):
        sga_ref[:, c0:c0 + cw] = jax.nn.sigmoid(proj("ga", c0, cw)).astype(BF16)
        sgb_ref[:, c0:c0 + cw] = jax.nn.sigmoid(proj("gb", c0, cw)).astype(BF16)

    wabt = wt_ref[offs["ab"]:offs["ab"] + gcst_ref.shape[0], :]
    gcst_ref[...] = _decay_and_beta(_dot_nt(wabt, hb), alogt[...], dtbt[...], 1, nh)


def _inproj(x2, scale, shift, norm_w, wt, offs, widths, alog, dtb, *, seq, nh):
    rows, d = x2.shape
    tm = INPROJ_ROWS
    assert seq % tm == 0 and tm % CHUNK == 0
    tiles_per_seq = seq // tm
    nab = widths["ab"]
    resident = lambda a: pl.BlockSpec(a.shape, lambda i: (0,) * a.ndim, pipeline_mode=pl.Buffered(1))
    consts = (norm_w, wt, alog.T, dtb.T)
    row_blk = lambda w: pl.BlockSpec((tm, w), lambda i: (i, 0))
    mod_blk = pl.BlockSpec((1, 1, d), lambda i: (i // tiles_per_seq, 0, 0))
    out_shapes = [jax.ShapeDtypeStruct((rows, widths[g]), BF16) for g in ("bg", "cg", "qkv", "z", "ga", "gb")]
    out_specs = [row_blk(s.shape[1]) for s in out_shapes]
    out_shapes += [jax.ShapeDtypeStruct((nab, rows), F32)]
    out_specs += [pl.BlockSpec((nab, tm), lambda i: (0, i))]
    return pl.pallas_call(
        functools.partial(_inproj_kernel, nh=nh, cw=INPROJ_COLS, offs=offs),
        grid=(rows // tm,),
        in_specs=[row_blk(d), mod_blk, mod_blk] + [resident(a) for a in consts],
        out_specs=out_specs,
        out_shape=out_shapes,
        compiler_params=pltpu.CompilerParams(dimension_semantics=("arbitrary",),
                                             vmem_limit_bytes=VMEM_LIMIT_BYTES),
        name="inproj",
    )(x2, scale, shift, *consts)


def _conv_rows(ext_ref, xm, prev, nxt, w_ref, width):
    r = xm.shape[0]
    half = width // 2
    pad = F32_SUBLANES
    ext_ref[pad:pad + r, :] = xm
    ext_ref[pad - half:pad, :] = prev[prev.shape[0] - half:, :]
    ext_ref[pad + r:pad + r + half, :] = nxt[:half, :]
    acc = None
    for t in range(width):
        term = ext_ref[pad - half + t:pad - half + t + r, :] * w_ref[t:t + 1, :]
        acc = term if acc is None else acc + term
    return acc


def _halo_specs(tile_rows, width_blk, col_of, row_tile_of, total_rows):
    per = tile_rows // BF16_SUBLANES
    last = total_rows // BF16_SUBLANES - 1

    def prev_map(*g):
        return (jnp.maximum(row_tile_of(*g) * per - 1, 0), col_of(*g))

    def next_map(*g):
        return (jnp.minimum((row_tile_of(*g) + 1) * per, last), col_of(*g))

    return (pl.BlockSpec((BF16_SUBLANES, width_blk), prev_map),
            pl.BlockSpec((BF16_SUBLANES, width_blk), next_map))


GROUP = 128


def _half_rows(a, b, second):
    start = b if second else 0
    return jnp.concatenate([a[k + start:k + start + b] for k in range(0, a.shape[0], 2 * b)], axis=0)


def _spread_rows(c, b, second, other):
    parts = []
    for k in range(0, c.shape[0], b):
        pair = (other[k:k + b], c[k:k + b])
        parts.extend(pair if second else pair[::-1])
    return jnp.concatenate(parts, axis=0)


def _zero_row_after(v):
    bits = pltpu.bitcast(v[:F32_SUBLANES, :HEAD_DIM], jnp.int32)
    zero = lax.shift_right_logical(lax.shift_right_logical(bits, 16), 16)
    return zero[:1].astype(F32)


def _block_inverses(pmats, lowers, ii, jj, fillers=()):
    same_block = lambda s: (ii >> s) == (jj >> s)
    eye = jnp.where(ii == jj, 1.0, 0.0)
    fillers = list(fillers)

    def fill(after):
        if fillers:
            fillers.pop(0)(_zero_row_after(after))

    base = F32_SUBLANES.bit_length() - 1
    ps = [jnp.where(same_block(base), p, 0.0) for p in pmats]
    xs = [eye + p for p in ps]
    pbs = [p.astype(BF16) for p in ps]
    p2s = [_dot(pb, pb) for pb in pbs]
    fill(p2s[-1])
    ys = [_dot(p2.astype(BF16), jnp.concatenate([x, p2], axis=1).astype(BF16)) for x, p2 in zip(xs, p2s)]
    xs = [x + y[:, :GROUP] for x, y in zip(xs, ys)]
    fill(ys[-1])
    xs = [x + _dot(y[:, GROUP:].astype(BF16), x.astype(BF16)) for x, y in zip(xs, ys)]

    zeros = jnp.zeros((GROUP // 2, GROUP), F32)
    for s in range(base, CHUNK.bit_length() - 1):
        b = 1 << s
        off = same_block(s + 1) & jnp.logical_not(same_block(s))
        offs = [jnp.where(off, p, 0.0) for p in pmats]
        xbs = [x.astype(BF16) for x in xs]
        ys = [_dot(_half_rows(o, b, lo).astype(BF16), xb) for o, xb, lo in zip(offs, xbs, lowers)]
        ups = [_dot(_half_rows(x, b, lo).astype(BF16), _spread_rows(y, b, lo, zeros).astype(BF16))
               for x, y, lo in zip(xs, ys, lowers)]
        xs = [_spread_rows(_half_rows(x, b, lo) + up, b, lo, _half_rows(x, b, not lo))
              for x, up, lo in zip(xs, ups, lowers)]
        fill(ys[-1])
    assert not fillers
    return xs


def _prep_kernel(qm, qp, qn, km, kp, kn, vm, vp, vn, cwq, cwk, cwv, gcst_ref,
                 lhs1_ref, lhs2_ref, u_ref, gl_ref, ext_ref, qkv_s, col_s, row_s, *, nrb, nh):
    i = pl.program_id(0)
    j = jnp.minimum(i, pl.num_programs(0) - 2)
    r = (j // nh) % nrb
    h = j % nh
    first = r == 0
    last = r == nrb - 1

    @pl.when(i == 0)
    def _():
        qkv_s[...] = jnp.zeros_like(qkv_s)
        col_s[...] = jnp.zeros_like(col_s)
        row_s[...] = jnp.zeros_like(row_s)

    q, k, v = qkv_s[0], qkv_s[1], qkv_s[2]
    g_col = (col_s[0], col_s[1])
    b_col = (col_s[2], col_s[3])
    nbc = (-b_col[0], -b_col[1])
    g_row = (row_s[0], row_s[1])

    def conv_silu(idx, m, p, n, cw, zero_row):
        pv = jnp.where(first, 0.0, p[...].astype(F32))
        nx = jnp.where(last, 0.0, n[...].astype(F32))
        w = cw[...] + zero_row
        return _silu(_conv_rows(ext_ref.at[idx], m[...].astype(F32), pv, nx, w, CONV_QKV_WIDTH))

    def stage1_q(zero_row):
        q1 = conv_silu(0, qm, qp, qn, cwq, zero_row)
        qkv_s[0] = q1 * (lax.rsqrt(jnp.sum(q1 * q1, axis=-1, keepdims=True) + L2_EPS) * (HEAD_DIM ** -0.5))

    def stage1_k(zero_row):
        k1 = conv_silu(1, km, kp, kn, cwk, zero_row)
        qkv_s[1] = k1 * lax.rsqrt(jnp.sum(k1 * k1, axis=-1, keepdims=True) + L2_EPS)

    def stage1_v(zero_row):
        qkv_s[2] = conv_silu(2, vm, vp, vn, cwv, zero_row)

    def stage1_gates(zero_row):
        del zero_row
        for slot in range(4):
            row = gcst_ref[pl.ds(slot * nh + h, 1), :]
            col_s[slot] = jnp.broadcast_to(row, (col_s.shape[2], col_s.shape[1])).T
        for d in range(N_DIR):
            row_s[d] = gcst_ref[pl.ds(d * nh + h, 1), :]

    ii = lax.broadcasted_iota(jnp.int32, (GROUP, GROUP), 0)
    jj = lax.broadcasted_iota(jnp.int32, (GROUP, GROUP), 1)
    same_chunk = (ii // CHUNK) == (jj // CHUNK)
    cpg = GROUP // CHUNK

    inst = []
    for gi in range(q.shape[0] // GROUP):
        rs = slice(gi * GROUP, (gi + 1) * GROUP)
        kc, qc, vc = k[rs], q[rs], v[rs]
        kq = _dot_nt(jnp.concatenate([kc, qc], axis=0).astype(BF16), kc.astype(BF16))
        kk, qk = kq[:GROUP], kq[GROUP:]
        for d in range(N_DIR):
            gc = g_col[d][rs]
            gr = g_row[d][:, rs]
            bc = b_col[d][rs]
            incl = same_chunk & ((ii >= jj) if d == 0 else (ii <= jj))
            strict = same_chunk & ((ii > jj) if d == 0 else (ii < jj))
            decay = jnp.exp2(gc - gr)
            pmat = jnp.where(strict, (nbc[d][rs] * kk) * decay, 0.0)
            aqk = jnp.where(incl, qk * decay, 0.0)
            eg = jnp.exp2(gc)
            edge = CHUNK - 1 if d == 0 else 0
            g_tot = jnp.concatenate(
                [jnp.broadcast_to(gc[c * CHUNK + edge:c * CHUNK + edge + 1], (CHUNK, HEAD_DIM))
                 for c in range(cpg)], axis=0)
            kdt = (kc * jnp.exp2(g_tot - gc)).T
            qd = qc * eg
            for c in range(cpg):
                cs = slice(c * CHUNK, (c + 1) * CHUNK)
                lhs1_ref[d, gi * cpg + c, CHUNK:, :] = qd[cs].astype(BF16)
                gl_ref[d, gi * cpg + c] = jnp.exp2(g_tot[c * CHUNK:c * CHUNK + 1])
            for p in range(cpg // 2):
                ls = slice(p * 2 * CHUNK, (p + 1) * 2 * CHUNK)
                a_pair = (aqk[p * 2 * CHUNK:p * 2 * CHUNK + CHUNK, ls]
                          + aqk[p * 2 * CHUNK + CHUNK:(p + 1) * 2 * CHUNK, ls])
                lhs2_ref[d, gi * (cpg // 2) + p] = jnp.concatenate([a_pair, kdt[:, ls]], axis=0).astype(BF16)
            rhs = jnp.concatenate([vc * bc, kc * (bc * eg)], axis=1).astype(BF16)
            inst.append(dict(gi=gi, d=d, pmat=pmat, rhs=rhs))

    tmats = _block_inverses([it["pmat"] for it in inst], [it["d"] == 0 for it in inst], ii, jj,
                            fillers=[stage1_gates, stage1_q, stage1_k, stage1_v])

    for it, tmat in zip(inst, tmats):
        gi, d = it["gi"], it["d"]
        uw = _dot(tmat.astype(BF16), it["rhs"])
        for c in range(cpg):
            cs = slice(c * CHUNK, (c + 1) * CHUNK)
            lhs1_ref[d, gi * cpg + c, :CHUNK, :] = uw[cs, HEAD_DIM:].astype(BF16)
            u_ref[d, gi * cpg + c] = uw[cs, :HEAD_DIM].astype(BF16)


def _prep(qkv, conv_w, gcst, *, bsz, seq, nh):
    rows = qkv.shape[0]
    rr = PREP_ROWS
    nrb = seq // rr
    ncb = rr // CHUNK
    nc = seq // CHUNK
    ngc = gcst.shape[0]
    n_tiles = bsz * nrb * nh
    assert GROUP == HEAD_DIM and rr % GROUP == 0

    tile_in = lambda i: jnp.minimum(i, n_tiles - 1)
    tile_out = lambda i: jnp.maximum(i - 1, 0)
    row_of = lambda i: tile_in(i) // nh
    head_of = lambda i: tile_in(i) % nh
    in_specs = []
    for part in range(3):
        col_of = functools.partial(lambda i, part: part * nh + head_of(i), part=part)
        in_specs.append(pl.BlockSpec((rr, HEAD_DIM), functools.partial(
            lambda i, col_of: (row_of(i), col_of(i)), col_of=col_of)))
        in_specs.extend(_halo_specs(rr, HEAD_DIM, col_of, row_of, rows))
    for part in range(3):
        in_specs.append(pl.BlockSpec((CONV_QKV_WIDTH, HEAD_DIM), functools.partial(
            lambda i, part: (0, part * nh + head_of(i)), part=part)))
    in_specs.append(pl.BlockSpec((ngc, rr), lambda i: (0, row_of(i))))

    def out(per_block, shape_tail, dtype):
        blk = (None, N_DIR, None, per_block) + shape_tail
        zeros = (0,) * len(shape_tail)

        def imap(i):
            t = tile_out(i)
            return (t // (nrb * nh), 0, t % nh, (t // nh) % nrb) + zeros

        return (jax.ShapeDtypeStruct((bsz, N_DIR, nh, per_block * nrb) + shape_tail, dtype),
                pl.BlockSpec(blk, imap))

    outs = [out(ncb, (2 * CHUNK, HEAD_DIM), BF16), out(ncb // 2, (CHUNK + HEAD_DIM, 2 * CHUNK), BF16),
            out(ncb, (CHUNK, HEAD_DIM), BF16), out(ncb, (1, HEAD_DIM), F32)]
    return pl.pallas_call(
        functools.partial(_prep_kernel, nrb=nrb, nh=nh),
        grid=(n_tiles + 1,),
        in_specs=in_specs,
        out_specs=[o[1] for o in outs],
        out_shape=[o[0] for o in outs],
        scratch_shapes=[pltpu.VMEM((3, rr + 2 * F32_SUBLANES, HEAD_DIM), F32),
                        pltpu.VMEM((3, rr, HEAD_DIM), F32),
                        pltpu.VMEM((4, rr, HEAD_DIM), F32),
                        pltpu.VMEM((N_DIR, 1, rr), F32)],
        compiler_params=pltpu.CompilerParams(dimension_semantics=("arbitrary",),
                                             vmem_limit_bytes=VMEM_LIMIT_BYTES),
        name="prep",
    )(qkv, qkv, qkv, qkv, qkv, qkv, qkv, qkv, qkv, conv_w, conv_w, conv_w, gcst)


def _scan_kernel(l1f, l2f, uf, glf, l1b, l2b, ub, glb, of_ref, ob_ref, s_ref, *, ncb, nh):
    @pl.when(pl.program_id(1) == 0)
    def _():
        s_ref[...] = jnp.zeros_like(s_ref)

    dirs = ((l1f, l2f, uf, glf, of_ref), (l1b, l2b, ub, glb, ob_ref))

    zero_rows = jnp.zeros((CHUNK, HEAD_DIM), BF16)

    def chunk_step(pair, parity):
        chains = [(d, h, pair[d], 2 * pair[d] + parity[d]) for d in range(N_DIR) for h in range(nh)]
        states = [s_ref[d, h] for d, h, _, _ in chains]
        r1s = [_dot(dirs[d][0][h, cc], s.astype(BF16))
               for (d, h, _, cc), s in zip(chains, states)]
        v_news = [dirs[d][2][h, cc].astype(F32) - r1[:CHUNK] for (d, h, _, cc), r1 in zip(chains, r1s)]
        r2s = []
        for (d, h, pc, _), v in zip(chains, v_news):
            vb = v.astype(BF16)
            rhs = jnp.concatenate([zero_rows, vb] if parity[d] else [vb, zero_rows], axis=0)
            r2s.append(_dot(dirs[d][1][h, pc], rhs))
        for (d, h, _, cc), s, r1, r2 in zip(chains, states, r1s, r2s):
            s_ref[d, h] = s * dirs[d][3][h, cc] + r2[CHUNK:]
            r0 = pl.multiple_of(cc * CHUNK, CHUNK)
            dirs[d][4][pl.ds(r0, CHUNK), h * HEAD_DIM:(h + 1) * HEAD_DIM] = (
                r1[CHUNK:] + r2[:CHUNK]).astype(dirs[d][4].dtype)

    def body(pi, carry):
        pair = (pi, ncb // 2 - 1 - pi)
        chunk_step(pair, (0, 1))
        chunk_step(pair, (1, 0))
        return carry

    lax.fori_loop(0, ncb // 2, body, 0)


def _scan(lhs1, lhs2, u, gl, *, seq):
    bsz, _, nh, nc = lhs1.shape[:4]
    rr = SCAN_ROWS
    ncb = rr // CHUNK
    nt = nc // ncb

    def specs(d):
        def mk(arr):
            tail = arr.shape[4:]
            zeros = (0,) * len(tail)
            if d == 0:
                imap = lambda b, t: (b, 0, 0, t) + zeros
            else:
                imap = lambda b, t: (b, 1, 0, nt - 1 - t) + zeros
            return pl.BlockSpec((None, None, nh, arr.shape[3] // nt) + tail, imap)
        return [mk(a) for a in (lhs1, lhs2, u, gl)]

    o_shape = jax.ShapeDtypeStruct((bsz * seq, nh * HEAD_DIM), BF16)
    return pl.pallas_call(
        functools.partial(_scan_kernel, ncb=ncb, nh=nh),
        grid=(bsz, nt),
        in_specs=specs(0) + specs(1),
        out_specs=[pl.BlockSpec((rr, nh * HEAD_DIM), lambda b, t: (b * nt + t, 0)),
                   pl.BlockSpec((rr, nh * HEAD_DIM), lambda b, t: (b * nt + nt - 1 - t, 0))],
        out_shape=[o_shape, o_shape],
        scratch_shapes=[pltpu.VMEM((N_DIR, nh, HEAD_DIM, HEAD_DIM), F32)],
        compiler_params=pltpu.CompilerParams(dimension_semantics=("arbitrary", "arbitrary"),
                                             vmem_limit_bytes=VMEM_LIMIT_BYTES),
        name="scan",
    )(lhs1, lhs2, u, gl, lhs1, lhs2, u, gl)


def _out_kernel(x_ref, of_ref, ob_ref, zs_ref, pm, pp, pn, gz_ref, sga_ref, sgb_ref, gate_ref,
                cwa_ref, gnw_ref, fnw_ref, wpa, wpb, wo, y_ref, ext_ref, *, tiles_per_seq, nh, final):
    r = pl.program_id(0) % tiles_per_seq
    sub = x_ref.shape[0] // OUT_SUBTILES
    rows = [slice(i * sub, (i + 1) * sub) for i in range(OUT_SUBTILES)]

    pv = jnp.where(r == 0, 0.0, pp[...].astype(F32))
    nx = jnp.where(r == tiles_per_seq - 1, 0.0, pn[...].astype(F32))
    conv = _conv_rows(ext_ref, pm[...].astype(F32), pv, nx, cwa_ref, CONV_A_WIDTH)

    def yb_input(rs):
        o = of_ref[rs, :].astype(F32) + ob_ref[rs, :].astype(F32)
        heads = []
        for h in range(nh):
            oh = o[:, h * HEAD_DIM:(h + 1) * HEAD_DIM]
            heads.append(oh * lax.rsqrt(jnp.mean(oh * oh, axis=-1, keepdims=True) + NORM_EPS) * gnw_ref[...])
        return (jnp.concatenate(heads, axis=1) * zs_ref[rs, :].astype(F32)).astype(BF16)

    yb_ins = [yb_input(rs) for rs in rows]
    ya_ins = [(gz_ref[rs, :].astype(F32) * conv[rs]).astype(BF16) for rs in rows]
    ybs = [_dot(v, wpb[...]) for v in yb_ins]
    yas = [_dot(v, wpa[...]) for v in ya_ins]
    mergeds = [(sga_ref[rs, :].astype(F32) * ya + sgb_ref[rs, :].astype(F32) * yb).astype(BF16)
               for rs, ya, yb in zip(rows, yas, ybs)]
    deltas = [_dot(m, wo[...]) for m in mergeds]
    for rs, delta in zip(rows, deltas):
        xn = x_ref[rs, :] + gate_ref[0] * delta
        if final:
            xn = xn * lax.rsqrt(jnp.mean(xn * xn, axis=-1, keepdims=True) + NORM_EPS) * fnw_ref[...]
        y_ref[rs, :] = xn


def _out(x2, o_f, o_b, zs, p, gz, sga, sgb, gate, conv_a_w, gdn_norm_w, final_norm_w, wpa, wpb, wo,
         *, seq, nh, final):
    rows, d = x2.shape
    tm = OUT_ROWS
    tiles_per_seq = seq // tm
    row_blk = lambda w: pl.BlockSpec((tm, w), lambda i: (i, 0))
    const = lambda a: pl.BlockSpec(a.shape, lambda i: (0,) * a.ndim)
    halo_prev, halo_next = _halo_specs(tm, p.shape[1], lambda i: 0, lambda i: i, rows)
    return pl.pallas_call(
        functools.partial(_out_kernel, tiles_per_seq=tiles_per_seq, nh=nh, final=final),
        grid=(rows // tm,),
        in_specs=[row_blk(d), row_blk(o_f.shape[1]), row_blk(o_b.shape[1]), row_blk(zs.shape[1]),
                  row_blk(p.shape[1]), halo_prev, halo_next, row_blk(gz.shape[1]),
                  row_blk(sga.shape[1]), row_blk(sgb.shape[1]),
                  pl.BlockSpec((1, 1, d), lambda i: (i // tiles_per_seq, 0, 0)),
                  const(conv_a_w), const(gdn_norm_w), const(final_norm_w),
                  const(wpa), const(wpb), const(wo)],
        out_specs=row_blk(d),
        out_shape=jax.ShapeDtypeStruct((rows, d), F32),
        scratch_shapes=[pltpu.VMEM((tm + 2 * F32_SUBLANES, p.shape[1]), F32)],
        compiler_params=pltpu.CompilerParams(dimension_semantics=("arbitrary",),
                                             vmem_limit_bytes=VMEM_LIMIT_BYTES),
        name="out",
    )(x2, o_f, o_b, zs, p, p, p, gz, sga, sgb, gate, conv_a_w, gdn_norm_w, final_norm_w, wpa, wpb, wo)


def kernel(x, c, w_ada, b_ada, norm_w, w_in, conv_a_w, conv_qkv_w, a_log, dt_bias, gdn_norm_w,
           w_pa, w_pb, w_o, final_norm_w):
    bsz, seq, d = x.shape
    depth = w_ada.shape[0]
    nh = a_log.shape[2]
    d_conv = conv_a_w.shape[2]
    key_dim = nh * HEAD_DIM
    assert conv_qkv_w.shape[2] == 3 * key_dim and gdn_norm_w.shape[1] == HEAD_DIM
    assert seq % PREP_ROWS == 0 and seq % SCAN_ROWS == 0 and seq % OUT_ROWS == 0
    n_gate = N_DIR * nh
    widths = dict(bg=d_conv, cg=d_conv, ax=d_conv, az=d_conv, qkv=3 * key_dim, z=key_dim,
                  ab=2 * n_gate, ga=d, gb=d)
    assert sum(widths.values()) == w_in.shape[2]
    offs, start = {}, 0
    for g, wd in widths.items():
        offs[g] = start
        start += wd
    assert all(o % BF16_SUBLANES == 0 for o in offs.values())

    x2 = x.reshape(bsz * seq, d)
    c_pad = jnp.zeros((8, d), F32).at[:bsz].set(c)
    for l in range(depth):
        mod = _ada(c_pad, w_ada[l], b_ada[l][None, :])[:bsz]
        shift, scale, gate = (mod[:, i * d:(i + 1) * d][:, None, :] for i in range(3))

        wt = w_in[l].T.astype(BF16)
        zeros = jnp.zeros((1, n_gate), F32)
        alog = jnp.concatenate([a_log[l].reshape(1, n_gate), zeros], axis=1)
        dtb = jnp.concatenate([dt_bias[l].reshape(1, n_gate), zeros], axis=1)
        p, gz, qkv, zs, sga, sgb, gcst = _inproj(
            x2, scale, shift, norm_w[l][None, :], wt, offs, widths, alog, dtb, seq=seq, nh=nh)

        lhs1, lhs2, u, gl = _prep(qkv, conv_qkv_w[l], gcst, bsz=bsz, seq=seq, nh=nh)
        o_f, o_b = _scan(lhs1, lhs2, u, gl, seq=seq)

        x2 = _out(x2, o_f, o_b, zs, p, gz, sga, sgb, gate, conv_a_w[l], gdn_norm_w[l][None, :],
                  final_norm_w[None, :], w_pa[l].astype(BF16), w_pb[l].astype(BF16), w_o[l].astype(BF16),
                  seq=seq, nh=nh, final=(l == depth - 1))
    return x2.reshape(bsz, seq, d)
```

```python
import functools

import jax
import jax.numpy as jnp
from jax import lax
from jax.experimental import pallas as pl
from jax.experimental.pallas import tpu as pltpu

F32 = jnp.float32
BF16 = jnp.bfloat16

HEAD_DIM = 128
CHUNK = 64
N_DIR = 2
CONV_A_WIDTH = 3
CONV_QKV_WIDTH = 5
NORM_EPS = 1e-6
L2_EPS = 1e-6
LOG2_E = 1.4426950408889634
BF16_SUBLANES = 16
F32_SUBLANES = 8
VMEM_LIMIT_BYTES = 56 * 1024 * 1024

INPROJ_ROWS = 512
INPROJ_COLS = 512
PREP_ROWS = 1024
SCAN_ROWS = 1024
OUT_ROWS = 512
OUT_SUBTILES = 2


def _dot(a, b):
    return jnp.dot(a, b, preferred_element_type=F32)


def _dot_nt(a, b):
    return lax.dot_general(a, b, (((1,), (1,)), ((), ())), preferred_element_type=F32)


def _silu(x):
    return x * jax.nn.sigmoid(x)


def _ada_kernel(c_ref, w_ref, b_ref, o_ref):
    sc = _silu(c_ref[...])
    o_ref[...] = jnp.dot(sc, w_ref[...], preferred_element_type=F32,
                         precision=lax.Precision.HIGHEST) + b_ref[...]


def _ada(c_pad, w, b):
    m, d = c_pad.shape
    n = w.shape[1]
    bn = d
    return pl.pallas_call(
        _ada_kernel,
        grid=(n // bn,),
        in_specs=[pl.BlockSpec((m, d), lambda j: (0, 0)),
                  pl.BlockSpec((d, bn), lambda j: (0, j)),
                  pl.BlockSpec((1, bn), lambda j: (0, j))],
        out_specs=pl.BlockSpec((m, bn), lambda j: (0, j)),
        out_shape=jax.ShapeDtypeStruct((m, n), F32),
        name="ada",
    )(c_pad, w, b)


def _seg_cumsum(x, axis, reverse):
    n = x.shape[axis]
    pos = lax.broadcasted_iota(jnp.int32, x.shape, axis) % CHUNK
    s = 1
    while s < CHUNK:
        if reverse:
            x = x + jnp.where(pos < CHUNK - s, pltpu.roll(x, n - s, axis), 0.0)
        else:
            x = x + jnp.where(pos >= s, pltpu.roll(x, s, axis), 0.0)
        s *= 2
    return x


def _decay_and_beta(ab, alog, dtb, axis, nh):
    pre = ab + dtb
    softplus = jnp.maximum(pre, 0.0) + jnp.log(1.0 + jnp.exp(-jnp.abs(pre)))
    g = -jnp.exp(alog) * softplus * LOG2_E
    beta = jax.nn.sigmoid(ab)
    col = lax.broadcasted_iota(jnp.int32, ab.shape, 1 - axis)
    gf = _seg_cumsum(g, axis, False)
    gb = _seg_cumsum(g, axis, True)
    return jnp.where(col < nh, gf, jnp.where(col < N_DIR * nh, gb, beta))


def _inproj_kernel(x_ref, sc_ref, sh_ref, nw_ref, wt_ref, alogt, dtbt,
                   p_ref, gz_ref, qkv_ref, zs_ref, sga_ref, sgb_ref, gcst_ref, *, nh, cw, offs):
    x = x_ref[...]
    h = x * lax.rsqrt(jnp.mean(x * x, axis=-1, keepdims=True) + NORM_EPS) * nw_ref[...]
    h = h * (1.0 + sc_ref[0]) + sh_ref[0]
    hb = h.astype(BF16)

    def proj(group, c0, width):
        start = offs[group] + c0
        return _dot_nt(hb, wt_ref[start:start + width, :])

    for c0 in range(0, p_ref.shape[1], cw):
        sl = slice(c0, c0 + cw)
        p_ref[:, sl] = (proj("bg", c0, cw) * proj("ax", c0, cw)).astype(BF16)
        gz_ref[:, sl] = (proj("cg", c0, cw) * _silu(proj("az", c0, cw))).astype(BF16)
    for c0 in range(0, qkv_ref.shape[1], cw):
        qkv_ref[:, c0:c0 + cw] = proj("qkv", c0, cw).astype(BF16)
    for c0 in range(0, zs_ref.shape[1], cw):
        zs_ref[:, c0:c0 + cw] = _silu(proj("z", c0, cw)).astype(BF16)
    for c0 in range(0, sga_ref.shape[1], cw):
        sga_ref[:, c0:c0 + cw] = jax.nn.sigmoid(proj("ga", c0, cw)).astype(BF16)
        sgb_ref[:, c0:c0 + cw] = jax.nn.sigmoid(proj("gb", c0, cw)).astype(BF16)

    wabt = wt_ref[offs["ab"]:offs["ab"] + gcst_ref.shape[0], :]
    gcst_ref[...] = _decay_and_beta(_dot_nt(wabt, hb), alogt[...], dtbt[...], 1, nh)


def _inproj(x2, scale, shift, norm_w, wt, offs, widths, alog, dtb, *, seq, nh):
    rows, d = x2.shape
    tm = INPROJ_ROWS
    assert seq % tm == 0 and tm % CHUNK == 0
    tiles_per_seq = seq // tm
    nab = widths["ab"]
    resident = lambda a: pl.BlockSpec(a.shape, lambda i: (0,) * a.ndim, pipeline_mode=pl.Buffered(1))
    consts = (norm_w, wt, alog.T, dtb.T)
    row_blk = lambda w: pl.BlockSpec((tm, w), lambda i: (i, 0))
    mod_blk = pl.BlockSpec((1, 1, d), lambda i: (i // tiles_per_seq, 0, 0))
    out_shapes = [jax.ShapeDtypeStruct((rows, widths[g]), BF16) for g in ("bg", "cg", "qkv", "z", "ga", "gb")]
    out_specs = [row_blk(s.shape[1]) for s in out_shapes]
    out_shapes += [jax.ShapeDtypeStruct((nab, rows), F32)]
    out_specs += [pl.BlockSpec((nab, tm), lambda i: (0, i))]
    return pl.pallas_call(
        functools.partial(_inproj_kernel, nh=nh, cw=INPROJ_COLS, offs=offs),
        grid=(rows // tm,),
        in_specs=[row_blk(d), mod_blk, mod_blk] + [resident(a) for a in consts],
        out_specs=out_specs,
        out_shape=out_shapes,
        compiler_params=pltpu.CompilerParams(dimension_semantics=("arbitrary",),
                                             vmem_limit_bytes=VMEM_LIMIT_BYTES),
        name="inproj",
    )(x2, scale, shift, *consts)


def _conv_rows(ext_ref, xm, prev, nxt, w_ref, width):
    r = xm.shape[0]
    half = width // 2
    pad = F32_SUBLANES
    ext_ref[pad:pad + r, :] = xm
    ext_ref[pad - half:pad, :] = prev[prev.shape[0] - half:, :]
    ext_ref[pad + r:pad + r + half, :] = nxt[:half, :]
    acc = None
    for t in range(width):
        term = ext_ref[pad - half + t:pad - half + t + r, :] * w_ref[t:t + 1, :]
        acc = term if acc is None else acc + term
    return acc


def _halo_specs(tile_rows, width_blk, col_of, row_tile_of, total_rows):
    per = tile_rows // BF16_SUBLANES
    last = total_rows // BF16_SUBLANES - 1

    def prev_map(*g):
        return (jnp.maximum(row_tile_of(*g) * per - 1, 0), col_of(*g))

    def next_map(*g):
        return (jnp.minimum((row_tile_of(*g) + 1) * per, last), col_of(*g))

    return (pl.BlockSpec((BF16_SUBLANES, width_blk), prev_map),
            pl.BlockSpec((BF16_SUBLANES, width_blk), next_map))


GROUP = 128


def _half_rows(a, b, second):
    start = b if second else 0
    return jnp.concatenate([a[k + start:k + start + b] for k in range(0, a.shape[0], 2 * b)], axis=0)


def _spread_rows(c, b, second, other):
    parts = []
    for k in range(0, c.shape[0], b):
        pair = (other[k:k + b], c[k:k + b])
        parts.extend(pair if second else pair[::-1])
    return jnp.concatenate(parts, axis=0)


def _zero_row_after(v):
    bits = pltpu.bitcast(v[:F32_SUBLANES, :HEAD_DIM], jnp.int32)
    zero = lax.shift_right_logical(lax.shift_right_logical(bits, 16), 16)
    return zero[:1].astype(F32)


def _block_inverses(pmats, lowers, ii, jj, fillers=()):
    same_block = lambda s: (ii >> s) == (jj >> s)
    eye = jnp.where(ii == jj, 1.0, 0.0)
    fillers = list(fillers)

    def fill(after):
        if fillers:
            fillers.pop(0)(_zero_row_after(after))

    base = F32_SUBLANES.bit_length() - 1
    ps = [jnp.where(same_block(base), p, 0.0) for p in pmats]
    xs = [eye + p for p in ps]
    pbs = [p.astype(BF16) for p in ps]
    p2s = [_dot(pb, pb) for pb in pbs]
    fill(p2s[-1])
    ys = [_dot(p2.astype(BF16), jnp.concatenate([x, p2], axis=1).astype(BF16)) for x, p2 in zip(xs, p2s)]
    xs = [x + y[:, :GROUP] for x, y in zip(xs, ys)]
    fill(ys[-1])
    xs = [x + _dot(y[:, GROUP:].astype(BF16), x.astype(BF16)) for x, y in zip(xs, ys)]

    zeros = jnp.zeros((GROUP // 2, GROUP), F32)
    for s in range(base, CHUNK.bit_length() - 1):
        b = 1 << s
        off = same_block(s + 1) & jnp.logical_not(same_block(s))
        offs = [jnp.where(off, p, 0.0) for p in pmats]
        xbs = [x.astype(BF16) for x in xs]
        ys = [_dot(_half_rows(o, b, lo).astype(BF16), xb) for o, xb, lo in zip(offs, xbs, lowers)]
        ups = [_dot(_half_rows(x, b, lo).astype(BF16), _spread_rows(y, b, lo, zeros).astype(BF16))
               for x, y, lo in zip(xs, ys, lowers)]
        xs = [_spread_rows(_half_rows(x, b, lo) + up, b, lo, _half_rows(x, b, not lo))
              for x, up, lo in zip(xs, ups, lowers)]
        fill(ys[-1])
    assert not fillers
    return xs


def _prep_kernel(qm, qp, qn, km, kp, kn, vm, vp, vn, cwq, cwk, cwv, gcst_ref,
                 lhs1_ref, lhs2_ref, u_ref, gl_ref, ext_ref, qkv_s, col_s, row_s, *, nrb, nh):
    i = pl.program_id(0)
    j = jnp.minimum(i, pl.num_programs(0) - 2)
    r = (j // nh) % nrb
    h = j % nh
    first = r == 0
    last = r == nrb - 1

    @pl.when(i == 0)
    def _():
        qkv_s[...] = jnp.zeros_like(qkv_s)
        col_s[...] = jnp.zeros_like(col_s)
        row_s[...] = jnp.zeros_like(row_s)

    q, k, v = qkv_s[0], qkv_s[1], qkv_s[2]
    g_col = (col_s[0], col_s[1])
    b_col = (col_s[2], col_s[3])
    nbc = (-b_col[0], -b_col[1])
    g_row = (row_s[0], row_s[1])

    def conv_silu(idx, m, p, n, cw, zero_row):
        pv = jnp.where(first, 0.0, p[...].astype(F32))
        nx = jnp.where(last, 0.0, n[...].astype(F32))
        w = cw[...] + zero_row
        return _silu(_conv_rows(ext_ref.at[idx], m[...].astype(F32), pv, nx, w, CONV_QKV_WIDTH))

    def stage1_q(zero_row):
        q1 = conv_silu(0, qm, qp, qn, cwq, zero_row)
        qkv_s[0] = q1 * (lax.rsqrt(jnp.sum(q1 * q1, axis=-1, keepdims=True) + L2_EPS) * (HEAD_DIM ** -0.5))

    def stage1_k(zero_row):
        k1 = conv_silu(1, km, kp, kn, cwk, zero_row)
        qkv_s[1] = k1 * lax.rsqrt(jnp.sum(k1 * k1, axis=-1, keepdims=True) + L2_EPS)

    def stage1_v(zero_row):
        qkv_s[2] = conv_silu(2, vm, vp, vn, cwv, zero_row)

    def stage1_gates(zero_row):
        del zero_row
        for slot in range(4):
            row = gcst_ref[pl.ds(slot * nh + h, 1), :]
            col_s[slot] = jnp.broadcast_to(row, (col_s.shape[2], col_s.shape[1])).T
        for d in range(N_DIR):
            row_s[d] = gcst_ref[pl.ds(d * nh + h, 1), :]

    ii = lax.broadcasted_iota(jnp.int32, (GROUP, GROUP), 0)
    jj = lax.broadcasted_iota(jnp.int32, (GROUP, GROUP), 1)
    same_chunk = (ii // CHUNK) == (jj // CHUNK)
    cpg = GROUP // CHUNK

    inst = []
    for gi in range(q.shape[0] // GROUP):
        rs = slice(gi * GROUP, (gi + 1) * GROUP)
        kc, qc, vc = k[rs], q[rs], v[rs]
        kq = _dot_nt(jnp.concatenate([kc, qc], axis=0).astype(BF16), kc.astype(BF16))
        kk, qk = kq[:GROUP], kq[GROUP:]
        for d in range(N_DIR):
            gc = g_col[d][rs]
            gr = g_row[d][:, rs]
            bc = b_col[d][rs]
            incl = same_chunk & ((ii >= jj) if d == 0 else (ii <= jj))
            strict = same_chunk & ((ii > jj) if d == 0 else (ii < jj))
            decay = jnp.exp2(gc - gr)
            pmat = jnp.where(strict, (nbc[d][rs] * kk) * decay, 0.0)
            aqk = jnp.where(incl, qk * decay, 0.0)
            eg = jnp.exp2(gc)
            edge = CHUNK - 1 if d == 0 else 0
            g_tot = jnp.concatenate(
                [jnp.broadcast_to(gc[c * CHUNK + edge:c * CHUNK + edge + 1], (CHUNK, HEAD_DIM))
                 for c in range(cpg)], axis=0)
            kdt = (kc * jnp.exp2(g_tot - gc)).T
            qd = qc * eg
            for c in range(cpg):
                cs = slice(c * CHUNK, (c + 1) * CHUNK)
                lhs1_ref[d, gi * cpg + c, CHUNK:, :] = qd[cs].astype(BF16)
                gl_ref[d, gi * cpg + c] = jnp.exp2(g_tot[c * CHUNK:c * CHUNK + 1])
            for p in range(cpg // 2):
                ls = slice(p * 2 * CHUNK, (p + 1) * 2 * CHUNK)
                a_pair = (aqk[p * 2 * CHUNK:p * 2 * CHUNK + CHUNK, ls]
                          + aqk[p * 2 * CHUNK + CHUNK:(p + 1) * 2 * CHUNK, ls])
                lhs2_ref[d, gi * (cpg // 2) + p] = jnp.concatenate([a_pair, kdt[:, ls]], axis=0).astype(BF16)
            rhs = jnp.concatenate([vc * bc, kc * (bc * eg)], axis=1).astype(BF16)
            inst.append(dict(gi=gi, d=d, pmat=pmat, rhs=rhs))

    tmats = _block_inverses([it["pmat"] for it in inst], [it["d"] == 0 for it in inst], ii, jj,
                            fillers=[stage1_gates, stage1_q, stage1_k, stage1_v])

    for it, tmat in zip(inst, tmats):
        gi, d = it["gi"], it["d"]
        uw = _dot(tmat.astype(BF16), it["rhs"])
        for c in range(cpg):
            cs = slice(c * CHUNK, (c + 1) * CHUNK)
            lhs1_ref[d, gi * cpg + c, :CHUNK, :] = uw[cs, HEAD_DIM:].astype(BF16)
            u_ref[d, gi * cpg + c] = uw[cs, :HEAD_DIM].astype(BF16)


def _prep(qkv, conv_w, gcst, *, bsz, seq, nh):
    rows = qkv.shape[0]
    rr = PREP_ROWS
    nrb = seq // rr
    ncb = rr // CHUNK
    nc = seq // CHUNK
    ngc = gcst.shape[0]
    n_tiles = bsz * nrb * nh
    assert GROUP == HEAD_DIM and rr % GROUP == 0

    tile_in = lambda i: jnp.minimum(i, n_tiles - 1)
    tile_out = lambda i: jnp.maximum(i - 1, 0)
    row_of = lambda i: tile_in(i) // nh
    head_of = lambda i: tile_in(i) % nh
    in_specs = []
    for part in range(3):
        col_of = functools.partial(lambda i, part: part * nh + head_of(i), part=part)
        in_specs.append(pl.BlockSpec((rr, HEAD_DIM), functools.partial(
            lambda i, col_of: (row_of(i), col_of(i)), col_of=col_of)))
        in_specs.extend(_halo_specs(rr, HEAD_DIM, col_of, row_of, rows))
    for part in range(3):
        in_specs.append(pl.BlockSpec((CONV_QKV_WIDTH, HEAD_DIM), functools.partial(
            lambda i, part: (0, part * nh + head_of(i)), part=part)))
    in_specs.append(pl.BlockSpec((ngc, rr), lambda i: (0, row_of(i))))

    def out(per_block, shape_tail, dtype):
        blk = (None, N_DIR, None, per_block) + shape_tail
        zeros = (0,) * len(shape_tail)

        def imap(i):
            t = tile_out(i)
            return (t // (nrb * nh), 0, t % nh, (t // nh) % nrb) + zeros

        return (jax.ShapeDtypeStruct((bsz, N_DIR, nh, per_block * nrb) + shape_tail, dtype),
                pl.BlockSpec(blk, imap))

    outs = [out(ncb, (2 * CHUNK, HEAD_DIM), BF16), out(ncb // 2, (CHUNK + HEAD_DIM, 2 * CHUNK), BF16),
            out(ncb, (CHUNK, HEAD_DIM), BF16), out(ncb, (1, HEAD_DIM), F32)]
    return pl.pallas_call(
        functools.partial(_prep_kernel, nrb=nrb, nh=nh),
        grid=(n_tiles + 1,),
        in_specs=in_specs,
        out_specs=[o[1] for o in outs],
        out_shape=[o[0] for o in outs],
        scratch_shapes=[pltpu.VMEM((3, rr + 2 * F32_SUBLANES, HEAD_DIM), F32),
                        pltpu.VMEM((3, rr, HEAD_DIM), F32),
                        pltpu.VMEM((4, rr, HEAD_DIM), F32),
                        pltpu.VMEM((N_DIR, 1, rr), F32)],
        compiler_params=pltpu.CompilerParams(dimension_semantics=("arbitrary",),
                                             vmem_limit_bytes=VMEM_LIMIT_BYTES),
        name="prep",
    )(qkv, qkv, qkv, qkv, qkv, qkv, qkv, qkv, qkv, conv_w, conv_w, conv_w, gcst)


def _scan_kernel(l1f, l2f, uf, glf, l1b, l2b, ub, glb, of_ref, ob_ref, s_ref, *, ncb, nh):
    @pl.when(pl.program_id(1) == 0)
    def _():
        s_ref[...] = jnp.zeros_like(s_ref)

    dirs = ((l1f, l2f, uf, glf, of_ref), (l1b, l2b, ub, glb, ob_ref))

    zero_rows = jnp.zeros((CHUNK, HEAD_DIM), BF16)

    def chunk_step(pair, parity):
        chains = [(d, h, pair[d], 2 * pair[d] + parity[d]) for d in range(N_DIR) for h in range(nh)]
        states = [s_ref[d, h] for d, h, _, _ in chains]
        r1s = [_dot(dirs[d][0][h, cc], s.astype(BF16))
               for (d, h, _, cc), s in zip(chains, states)]
        v_news = [dirs[d][2][h, cc].astype(F32) - r1[:CHUNK] for (d, h, _, cc), r1 in zip(chains, r1s)]
        r2s = []
        for (d, h, pc, _), v in zip(chains, v_news):
            vb = v.astype(BF16)
            rhs = jnp.concatenate([zero_rows, vb] if parity[d] else [vb, zero_rows], axis=0)
            r2s.append(_dot(dirs[d][1][h, pc], rhs))
        for (d, h, _, cc), s, r1, r2 in zip(chains, states, r1s, r2s):
            s_ref[d, h] = s * dirs[d][3][h, cc] + r2[CHUNK:]
            r0 = pl.multiple_of(cc * CHUNK, CHUNK)
            dirs[d][4][pl.ds(r0, CHUNK), h * HEAD_DIM:(h + 1) * HEAD_DIM] = (
                r1[CHUNK:] + r2[:CHUNK]).astype(dirs[d][4].dtype)

    def body(pi, carry):
        pair = (pi, ncb // 2 - 1 - pi)
        chunk_step(pair, (0, 1))
        chunk_step(pair, (1, 0))
        return carry

    lax.fori_loop(0, ncb // 2, body, 0)


def _scan(lhs1, lhs2, u, gl, *, seq):
    bsz, _, nh, nc = lhs1.shape[:4]
    rr = SCAN_ROWS
    ncb = rr // CHUNK
    nt = nc // ncb

    def specs(d):
        def mk(arr):
            tail = arr.shape[4:]
            zeros = (0,) * len(tail)
            if d == 0:
                imap = lambda b, t: (b, 0, 0, t) + zeros
            else:
                imap = lambda b, t: (b, 1, 0, nt - 1 - t) + zeros
            return pl.BlockSpec((None, None, nh, arr.shape[3] // nt) + tail, imap)
        return [mk(a) for a in (lhs1, lhs2, u, gl)]

    o_shape = jax.ShapeDtypeStruct((bsz * seq, nh * HEAD_DIM), BF16)
    return pl.pallas_call(
        functools.partial(_scan_kernel, ncb=ncb, nh=nh),
        grid=(bsz, nt),
        in_specs=specs(0) + specs(1),
        out_specs=[pl.BlockSpec((rr, nh * HEAD_DIM), lambda b, t: (b * nt + t, 0)),
                   pl.BlockSpec((rr, nh * HEAD_DIM), lambda b, t: (b * nt + nt - 1 - t, 0))],
        out_shape=[o_shape, o_shape],
        scratch_shapes=[pltpu.VMEM((N_DIR, nh, HEAD_DIM, HEAD_DIM), F32)],
        compiler_params=pltpu.CompilerParams(dimension_semantics=("arbitrary", "arbitrary"),
                                             vmem_limit_bytes=VMEM_LIMIT_BYTES),
        name="scan",
    )(lhs1, lhs2, u, gl, lhs1, lhs2, u, gl)


def _out_kernel(x_ref, of_ref, ob_ref, zs_ref, pm, pp, pn, gz_ref, sga_ref, sgb_ref, gate_ref,
                cwa_ref, gnw_ref, fnw_ref, wpa, wpb, wo, y_ref, ext_ref, *, tiles_per_seq, nh, final):
    r = pl.program_id(0) % tiles_per_seq
    sub = x_ref.shape[0] // OUT_SUBTILES
    rows = [slice(i * sub, (i + 1) * sub) for i in range(OUT_SUBTILES)]

    pv = jnp.where(r == 0, 0.0, pp[...].astype(F32))
    nx = jnp.where(r == tiles_per_seq - 1, 0.0, pn[...].astype(F32))
    conv = _conv_rows(ext_ref, pm[...].astype(F32), pv, nx, cwa_ref, CONV_A_WIDTH)

    def yb_input(rs):
        o = of_ref[rs, :].astype(F32) + ob_ref[rs, :].astype(F32)
        heads = []
        for h in range(nh):
            oh = o[:, h * HEAD_DIM:(h + 1) * HEAD_DIM]
            heads.append(oh * lax.rsqrt(jnp.mean(oh * oh, axis=-1, keepdims=True) + NORM_EPS) * gnw_ref[...])
        return (jnp.concatenate(heads, axis=1) * zs_ref[rs, :].astype(F32)).astype(BF16)

    yb_ins = [yb_input(rs) for rs in rows]
    ya_ins = [(gz_ref[rs, :].astype(F32) * conv[rs]).astype(BF16) for rs in rows]
    ybs = [_dot(v, wpb[...]) for v in yb_ins]
    yas = [_dot(v, wpa[...]) for v in ya_ins]
    mergeds = [(sga_ref[rs, :].astype(F32) * ya + sgb_ref[rs, :].astype(F32) * yb).astype(BF16)
               for rs, ya, yb in zip(rows, yas, ybs)]
    deltas = [_dot(m, wo[...]) for m in mergeds]
    for rs, delta in zip(rows, deltas):
        xn = x_ref[rs, :] + gate_ref[0] * delta
        if final:
            xn = xn * lax.rsqrt(jnp.mean(xn * xn, axis=-1, keepdims=True) + NORM_EPS) * fnw_ref[...]
        y_ref[rs, :] = xn


def _out(x2, o_f, o_b, zs, p, gz, sga, sgb, gate, conv_a_w, gdn_norm_w, final_norm_w, wpa, wpb, wo,
         *, seq, nh, final):
    rows, d = x2.shape
    tm = OUT_ROWS
    tiles_per_seq = seq // tm
    row_blk = lambda w: pl.BlockSpec((tm, w), lambda i: (i, 0))
    const = lambda a: pl.BlockSpec(a.shape, lambda i: (0,) * a.ndim)
    halo_prev, halo_next = _halo_specs(tm, p.shape[1], lambda i: 0, lambda i: i, rows)
    return pl.pallas_call(
        functools.partial(_out_kernel, tiles_per_seq=tiles_per_seq, nh=nh, final=final),
        grid=(rows // tm,),
        in_specs=[row_blk(d), row_blk(o_f.shape[1]), row_blk(o_b.shape[1]), row_blk(zs.shape[1]),
                  row_blk(p.shape[1]), halo_prev, halo_next, row_blk(gz.shape[1]),
                  row_blk(sga.shape[1]), row_blk(sgb.shape[1]),
                  pl.BlockSpec((1, 1, d), lambda i: (i // tiles_per_seq, 0, 0)),
                  const(conv_a_w), const(gdn_norm_w), const(final_norm_w),
                  const(wpa), const(wpb), const(wo)],
        out_specs=row_blk(d),
        out_shape=jax.ShapeDtypeStruct((rows, d), F32),
        scratch_shapes=[pltpu.VMEM((tm + 2 * F32_SUBLANES, p.shape[1]), F32)],
        compiler_params=pltpu.CompilerParams(dimension_semantics=("arbitrary",),
                                             vmem_limit_bytes=VMEM_LIMIT_BYTES),
        name="out",
    )(x2, o_f, o_b, zs, p, p, p, gz, sga, sgb, gate, conv_a_w, gdn_norm_w, final_norm_w, wpa, wpb, wo)


def kernel(x, c, w_ada, b_ada, norm_w, w_in, conv_a_w, conv_qkv_w, a_log, dt_bias, gdn_norm_w,
           w_pa, w_pb, w_o, final_norm_w):
    bsz, seq, d = x.shape
    depth = w_ada.shape[0]
    nh = a_log.shape[2]
    d_conv = conv_a_w.shape[2]
    key_dim = nh * HEAD_DIM
    assert conv_qkv_w.shape[2] == 3 * key_dim and gdn_norm_w.shape[1] == HEAD_DIM
    assert seq % PREP_ROWS == 0 and seq % SCAN_ROWS == 0 and seq % OUT_ROWS == 0
    n_gate = N_DIR * nh
    widths = dict(bg=d_conv, cg=d_conv, ax=d_conv, az=d_conv, qkv=3 * key_dim, z=key_dim,
                  ab=2 * n_gate, ga=d, gb=d)
    assert sum(widths.values()) == w_in.shape[2]
    offs, start = {}, 0
    for g, wd in widths.items():
        offs[g] = start
        start += wd
    assert all(o % BF16_SUBLANES == 0 for o in offs.values())

    x2 = x.reshape(bsz * seq, d)
    c_pad = jnp.zeros((8, d), F32).at[:bsz].set(c)
    for l in range(depth):
        mod = _ada(c_pad, w_ada[l], b_ada[l][None, :])[:bsz]
        shift, scale, gate = (mod[:, i * d:(i + 1) * d][:, None, :] for i in range(3))

        wt = w_in[l].T.astype(BF16)
        zeros = jnp.zeros((1, n_gate), F32)
        alog = jnp.concatenate([a_log[l].reshape(1, n_gate), zeros], axis=1)
        dtb = jnp.concatenate([dt_bias[l].reshape(1, n_gate), zeros], axis=1)
        p, gz, qkv, zs, sga, sgb, gcst = _inproj(
            x2, scale, shift, norm_w[l][None, :], wt, offs, widths, alog, dtb, seq=seq, nh=nh)

        lhs1, lhs2, u, gl = _prep(qkv, conv_qkv_w[l], gcst, bsz=bsz, seq=seq, nh=nh)
        o_f, o_b = _scan(lhs1, lhs2, u, gl, seq=seq)

        x2 = _out(x2, o_f, o_b, zs, p, gz, sga, sgb, gate, conv_a_w[l], gdn_norm_w[l][None, :],
                  final_norm_w[None, :], w_pa[l].astype(BF16), w_pb[l].astype(BF16), w_o[l].astype(BF16),
                  seq=seq, nh=nh, final=(l == depth - 1))
    return x2.reshape(bsz, seq, d)
```

```python
import functools

import jax
import jax.numpy as jnp
from jax import lax
from jax.experimental import pallas as pl
from jax.experimental.pallas import tpu as pltpu

F32 = jnp.float32
BF16 = jnp.bfloat16

HEAD_DIM = 128
CHUNK = 64
N_DIR = 2
CONV_A_WIDTH = 3
CONV_QKV_WIDTH = 5
NORM_EPS = 1e-6
L2_EPS = 1e-6
LOG2_E = 1.4426950408889634
BF16_SUBLANES = 16
F32_SUBLANES = 8
VMEM_LIMIT_BYTES = 56 * 1024 * 1024

INPROJ_ROWS = 512
INPROJ_COLS = 512
PREP_ROWS = 1024
SCAN_ROWS = 1024
OUT_ROWS = 512
OUT_SUBTILES = 2


def _dot(a, b):
    return jnp.dot(a, b, preferred_element_type=F32)


def _dot_nt(a, b):
    return lax.dot_general(a, b, (((1,), (1,)), ((), ())), preferred_element_type=F32)


def _silu(x):
    return x * jax.nn.sigmoid(x)


def _ada_kernel(c_ref, w_ref, b_ref, o_ref):
    sc = _silu(c_ref[...])
    o_ref[...] = jnp.dot(sc, w_ref[...], preferred_element_type=F32,
                         precision=lax.Precision.HIGHEST) + b_ref[...]


def _ada(c_pad, w, b):
    m, d = c_pad.shape
    n = w.shape[1]
    bn = d
    return pl.pallas_call(
        _ada_kernel,
        grid=(n // bn,),
        in_specs=[pl.BlockSpec((m, d), lambda j: (0, 0)),
                  pl.BlockSpec((d, bn), lambda j: (0, j)),
                  pl.BlockSpec((1, bn), lambda j: (0, j))],
        out_specs=pl.BlockSpec((m, bn), lambda j: (0, j)),
        out_shape=jax.ShapeDtypeStruct((m, n), F32),
        name="ada",
    )(c_pad, w, b)


def _seg_cumsum(x, axis, reverse):
    n = x.shape[axis]
    pos = lax.broadcasted_iota(jnp.int32, x.shape, axis) % CHUNK
    s = 1
    while s < CHUNK:
        if reverse:
            x = x + jnp.where(pos < CHUNK - s, pltpu.roll(x, n - s, axis), 0.0)
        else:
            x = x + jnp.where(pos >= s, pltpu.roll(x, s, axis), 0.0)
        s *= 2
    return x


def _decay_and_beta(ab, alog, dtb, axis, nh):
    pre = ab + dtb
    softplus = jnp.maximum(pre, 0.0) + jnp.log(1.0 + jnp.exp(-jnp.abs(pre)))
    g = -jnp.exp(alog) * softplus * LOG2_E
    beta = jax.nn.sigmoid(ab)
    col = lax.broadcasted_iota(jnp.int32, ab.shape, 1 - axis)
    gf = _seg_cumsum(g, axis, False)
    gb = _seg_cumsum(g, axis, True)
    return jnp.where(col < nh, gf, jnp.where(col < N_DIR * nh, gb, beta))


def _inproj_kernel(x_ref, sc_ref, sh_ref, nw_ref, wt_ref, alogt, dtbt,
                   p_ref, gz_ref, qkv_ref, zs_ref, sga_ref, sgb_ref, gcst_ref, *, nh, cw, offs):
    x = x_ref[...]
    h = x * lax.rsqrt(jnp.mean(x * x, axis=-1, keepdims=True) + NORM_EPS) * nw_ref[...]
    h = h * (1.0 + sc_ref[0]) + sh_ref[0]
    hb = h.astype(BF16)

    def proj(group, c0, width):
        start = offs[group] + c0
        return _dot_nt(hb, wt_ref[start:start + width, :])

    for c0 in range(0, p_ref.shape[1], cw):
        sl = slice(c0, c0 + cw)
        p_ref[:, sl] = (proj("bg", c0, cw) * proj("ax", c0, cw)).astype(BF16)
        gz_ref[:, sl] = (proj("cg", c0, cw) * _silu(proj("az", c0, cw))).astype(BF16)
    for c0 in range(0, qkv_ref.shape[1], cw):
        qkv_ref[:, c0:c0 + cw] = proj("qkv", c0, cw).astype(BF16)
    for c0 in range(0, zs_ref.shape[1], cw):
        zs_ref[:, c0:c0 + cw] = _silu(proj("z", c0, cw)).astype(BF16)
    for c0 in range(0, sga_ref.shape[1], cw):
        sga_ref[:, c0:c0 + cw] = jax.nn.sigmoid(proj("ga", c0, cw)).astype(BF16)
        sgb_ref[:, c0:c0 + cw] = jax.nn.sigmoid(proj("gb", c0, cw)).astype(BF16)

    wabt = wt_ref[offs["ab"]:offs["ab"] + gcst_ref.shape[0], :]
    gcst_ref[...] = _decay_and_beta(_dot_nt(wabt, hb), alogt[...], dtbt[...], 1, nh)


def _inproj(x2, scale, shift, norm_w, wt, offs, widths, alog, dtb, *, seq, nh):
    rows, d = x2.shape
    tm = INPROJ_ROWS
    assert seq % tm == 0 and tm % CHUNK == 0
    tiles_per_seq = seq // tm
    nab = widths["ab"]
    resident = lambda a: pl.BlockSpec(a.shape, lambda i: (0,) * a.ndim, pipeline_mode=pl.Buffered(1))
    consts = (norm_w, wt, alog.T, dtb.T)
    row_blk = lambda w: pl.BlockSpec((tm, w), lambda i: (i, 0))
    mod_blk = pl.BlockSpec((1, 1, d), lambda i: (i // tiles_per_seq, 0, 0))
    out_shapes = [jax.ShapeDtypeStruct((rows, widths[g]), BF16) for g in ("bg", "cg", "qkv", "z", "ga", "gb")]
    out_specs = [row_blk(s.shape[1]) for s in out_shapes]
    out_shapes += [jax.ShapeDtypeStruct((nab, rows), F32)]
    out_specs += [pl.BlockSpec((nab, tm), lambda i: (0, i))]
    return pl.pallas_call(
        functools.partial(_inproj_kernel, nh=nh, cw=INPROJ_COLS, offs=offs),
        grid=(rows // tm,),
        in_specs=[row_blk(d), mod_blk, mod_blk] + [resident(a) for a in consts],
        out_specs=out_specs,
        out_shape=out_shapes,
        compiler_params=pltpu.CompilerParams(dimension_semantics=("arbitrary",),
                                             vmem_limit_bytes=VMEM_LIMIT_BYTES),
        name="inproj",
    )(x2, scale, shift, *consts)


def _conv_rows(ext_ref, xm, prev, nxt, w_ref, width):
    r = xm.shape[0]
    half = width // 2
    pad = F32_SUBLANES
    ext_ref[pad:pad + r, :] = xm
    ext_ref[pad - half:pad, :] = prev[prev.shape[0] - half:, :]
    ext_ref[pad + r:pad + r + half, :] = nxt[:half, :]
    acc = None
    for t in range(width):
        term = ext_ref[pad - half + t:pad - half + t + r, :] * w_ref[t:t + 1, :]
        acc = term if acc is None else acc + term
    return acc


def _halo_specs(tile_rows, width_blk, col_of, row_tile_of, total_rows):
    per = tile_rows // BF16_SUBLANES
    last = total_rows // BF16_SUBLANES - 1

    def prev_map(*g):
        return (jnp.maximum(row_tile_of(*g) * per - 1, 0), col_of(*g))

    def next_map(*g):
        return (jnp.minimum((row_tile_of(*g) + 1) * per, last), col_of(*g))

    return (pl.BlockSpec((BF16_SUBLANES, width_blk), prev_map),
            pl.BlockSpec((BF16_SUBLANES, width_blk), next_map))


GROUP = 128


def _half_rows(a, b, second):
    start = b if second else 0
    return jnp.concatenate([a[k + start:k + start + b] for k in range(0, a.shape[0], 2 * b)], axis=0)


def _spread_rows(c, b, second, other):
    parts = []
    for k in range(0, c.shape[0], b):
        pair = (other[k:k + b], c[k:k + b])
        parts.extend(pair if second else pair[::-1])
    return jnp.concatenate(parts, axis=0)


def _zero_row_after(v):
    bits = pltpu.bitcast(v[:F32_SUBLANES, :HEAD_DIM], jnp.int32)
    zero = lax.shift_right_logical(lax.shift_right_logical(bits, 16), 16)
    return zero[:1].astype(F32)


def _block_inverses(pmats, lowers, ii, jj, fillers=()):
    same_block = lambda s: (ii >> s) == (jj >> s)
    eye = jnp.where(ii == jj, 1.0, 0.0)
    fillers = list(fillers)

    def fill(after):
        if fillers:
            fillers.pop(0)(_zero_row_after(after))

    base = F32_SUBLANES.bit_length() - 1
    ps = [jnp.where(same_block(base), p, 0.0) for p in pmats]
    xs = [eye + p for p in ps]
    pbs = [p.astype(BF16) for p in ps]
    p2s = [_dot(pb, pb) for pb in pbs]
    fill(p2s[-1])
    ys = [_dot(p2.astype(BF16), jnp.concatenate([x, p2], axis=1).astype(BF16)) for x, p2 in zip(xs, p2s)]
    xs = [x + y[:, :GROUP] for x, y in zip(xs, ys)]
    fill(ys[-1])
    xs = [x + _dot(y[:, GROUP:].astype(BF16), x.astype(BF16)) for x, y in zip(xs, ys)]

    zeros = jnp.zeros((GROUP // 2, GROUP), F32)
    for s in range(base, CHUNK.bit_length() - 1):
        b = 1 << s
        off = same_block(s + 1) & jnp.logical_not(same_block(s))
        offs = [jnp.where(off, p, 0.0) for p in pmats]
        xbs = [x.astype(BF16) for x in xs]
        ys = [_dot(_half_rows(o, b, lo).astype(BF16), xb) for o, xb, lo in zip(offs, xbs, lowers)]
        ups = [_dot(_half_rows(x, b, lo).astype(BF16), _spread_rows(y, b, lo, zeros).astype(BF16))
               for x, y, lo in zip(xs, ys, lowers)]
        xs = [_spread_rows(_half_rows(x, b, lo) + up, b, lo, _half_rows(x, b, not lo))
              for x, up, lo in zip(xs, ups, lowers)]
        fill(ys[-1])
    assert not fillers
    return xs


def _prep_kernel(qm, qp, qn, km, kp, kn, vm, vp, vn, cwq, cwk, cwv, gcst_ref,
                 lhs1_ref, lhs2_ref, u_ref, gl_ref, ext_ref, qkv_s, col_s, row_s, *, nrb, nh):
    i = pl.program_id(0)
    j = jnp.minimum(i, pl.num_programs(0) - 2)
    r = (j // nh) % nrb
    h = j % nh
    first = r == 0
    last = r == nrb - 1

    @pl.when(i == 0)
    def _():
        qkv_s[...] = jnp.zeros_like(qkv_s)
        col_s[...] = jnp.zeros_like(col_s)
        row_s[...] = jnp.zeros_like(row_s)

    q, k, v = qkv_s[0], qkv_s[1], qkv_s[2]
    g_col = (col_s[0], col_s[1])
    b_col = (col_s[2], col_s[3])
    nbc = (-b_col[0], -b_col[1])
    g_row = (row_s[0], row_s[1])

    def conv_silu(idx, m, p, n, cw, zero_row):
        pv = jnp.where(first, 0.0, p[...].astype(F32))
        nx = jnp.where(last, 0.0, n[...].astype(F32))
        w = cw[...] + zero_row
        return _silu(_conv_rows(ext_ref.at[idx], m[...].astype(F32), pv, nx, w, CONV_QKV_WIDTH))

    def stage1_q(zero_row):
        q1 = conv_silu(0, qm, qp, qn, cwq, zero_row)
        qkv_s[0] = q1 * (lax.rsqrt(jnp.sum(q1 * q1, axis=-1, keepdims=True) + L2_EPS) * (HEAD_DIM ** -0.5))

    def stage1_k(zero_row):
        k1 = conv_silu(1, km, kp, kn, cwk, zero_row)
        qkv_s[1] = k1 * lax.rsqrt(jnp.sum(k1 * k1, axis=-1, keepdims=True) + L2_EPS)

    def stage1_v(zero_row):
        qkv_s[2] = conv_silu(2, vm, vp, vn, cwv, zero_row)

    def stage1_gates(zero_row):
        del zero_row
        for slot in range(4):
            row = gcst_ref[pl.ds(slot * nh + h, 1), :]
            col_s[slot] = jnp.broadcast_to(row, (col_s.shape[2], col_s.shape[1])).T
        for d in range(N_DIR):
            row_s[d] = gcst_ref[pl.ds(d * nh + h, 1), :]

    ii = lax.broadcasted_iota(jnp.int32, (GROUP, GROUP), 0)
    jj = lax.broadcasted_iota(jnp.int32, (GROUP, GROUP), 1)
    same_chunk = (ii // CHUNK) == (jj // CHUNK)
    cpg = GROUP // CHUNK

    inst = []
    for gi in range(q.shape[0] // GROUP):
        rs = slice(gi * GROUP, (gi + 1) * GROUP)
        kc, qc, vc = k[rs], q[rs], v[rs]
        kq = _dot_nt(jnp.concatenate([kc, qc], axis=0).astype(BF16), kc.astype(BF16))
        kk, qk = kq[:GROUP], kq[GROUP:]
        for d in range(N_DIR):
            gc = g_col[d][rs]
            gr = g_row[d][:, rs]
            bc = b_col[d][rs]
            incl = same_chunk & ((ii >= jj) if d == 0 else (ii <= jj))
            strict = same_chunk & ((ii > jj) if d == 0 else (ii < jj))
            decay = jnp.exp2(gc - gr)
            pmat = jnp.where(strict, (nbc[d][rs] * kk) * decay, 0.0)
            aqk = jnp.where(incl, qk * decay, 0.0)
            eg = jnp.exp2(gc)
            edge = CHUNK - 1 if d == 0 else 0
            g_tot = jnp.concatenate(
                [jnp.broadcast_to(gc[c * CHUNK + edge:c * CHUNK + edge + 1], (CHUNK, HEAD_DIM))
                 for c in range(cpg)], axis=0)
            kdt = (kc * jnp.exp2(g_tot - gc)).T
            qd = qc * eg
            for c in range(cpg):
                cs = slice(c * CHUNK, (c + 1) * CHUNK)
                lhs1_ref[d, gi * cpg + c, CHUNK:, :] = qd[cs].astype(BF16)
                gl_ref[d, gi * cpg + c] = jnp.exp2(g_tot[c * CHUNK:c * CHUNK + 1])
            for p in range(cpg // 2):
                ls = slice(p * 2 * CHUNK, (p + 1) * 2 * CHUNK)
                a_pair = (aqk[p * 2 * CHUNK:p * 2 * CHUNK + CHUNK, ls]
                          + aqk[p * 2 * CHUNK + CHUNK:(p + 1) * 2 * CHUNK, ls])
                lhs2_ref[d, gi * (cpg // 2) + p] = jnp.concatenate([a_pair, kdt[:, ls]], axis=0).astype(BF16)
            rhs = jnp.concatenate([vc * bc, kc * (bc * eg)], axis=1).astype(BF16)
            inst.append(dict(gi=gi, d=d, pmat=pmat, rhs=rhs))

    tmats = _block_inverses([it["pmat"] for it in inst], [it["d"] == 0 for it in inst], ii, jj,
                            fillers=[stage1_gates, stage1_q, stage1_k, stage1_v])

    for it, tmat in zip(inst, tmats):
        gi, d = it["gi"], it["d"]
        uw = _dot(tmat.astype(BF16), it["rhs"])
        for c in range(cpg):
            cs = slice(c * CHUNK, (c + 1) * CHUNK)
            lhs1_ref[d, gi * cpg + c, :CHUNK, :] = uw[cs, HEAD_DIM:].astype(BF16)
            u_ref[d, gi * cpg + c] = uw[cs, :HEAD_DIM].astype(BF16)


def _prep(qkv, conv_w, gcst, *, bsz, seq, nh):
    rows = qkv.shape[0]
    rr = PREP_ROWS
    nrb = seq // rr
    ncb = rr // CHUNK
    nc = seq // CHUNK
    ngc = gcst.shape[0]
    n_tiles = bsz * nrb * nh
    assert GROUP == HEAD_DIM and rr % GROUP == 0

    tile_in = lambda i: jnp.minimum(i, n_tiles - 1)
    tile_out = lambda i: jnp.maximum(i - 1, 0)
    row_of = lambda i: tile_in(i) // nh
    head_of = lambda i: tile_in(i) % nh
    in_specs = []
    for part in range(3):
        col_of = functools.partial(lambda i, part: part * nh + head_of(i), part=part)
        in_specs.append(pl.BlockSpec((rr, HEAD_DIM), functools.partial(
            lambda i, col_of: (row_of(i), col_of(i)), col_of=col_of)))
        in_specs.extend(_halo_specs(rr, HEAD_DIM, col_of, row_of, rows))
    for part in range(3):
        in_specs.append(pl.BlockSpec((CONV_QKV_WIDTH, HEAD_DIM), functools.partial(
            lambda i, part: (0, part * nh + head_of(i)), part=part)))
    in_specs.append(pl.BlockSpec((ngc, rr), lambda i: (0, row_of(i))))

    def out(per_block, shape_tail, dtype):
        blk = (None, N_DIR, None, per_block) + shape_tail
        zeros = (0,) * len(shape_tail)

        def imap(i):
            t = tile_out(i)
            return (t // (nrb * nh), 0, t % nh, (t // nh) % nrb) + zeros

        return (jax.ShapeDtypeStruct((bsz, N_DIR, nh, per_block * nrb) + shape_tail, dtype),
                pl.BlockSpec(blk, imap))

    outs = [out(ncb, (2 * CHUNK, HEAD_DIM), BF16), out(ncb // 2, (CHUNK + HEAD_DIM, 2 * CHUNK), BF16),
            out(ncb, (CHUNK, HEAD_DIM), BF16), out(ncb, (1, HEAD_DIM), F32)]
    return pl.pallas_call(
        functools.partial(_prep_kernel, nrb=nrb, nh=nh),
        grid=(n_tiles + 1,),
        in_specs=in_specs,
        out_specs=[o[1] for o in outs],
        out_shape=[o[0] for o in outs],
        scratch_shapes=[pltpu.VMEM((3, rr + 2 * F32_SUBLANES, HEAD_DIM), F32),
                        pltpu.VMEM((3, rr, HEAD_DIM), F32),
                        pltpu.VMEM((4, rr, HEAD_DIM), F32),
                        pltpu.VMEM((N_DIR, 1, rr), F32)],
        compiler_params=pltpu.CompilerParams(dimension_semantics=("arbitrary",),
                                             vmem_limit_bytes=VMEM_LIMIT_BYTES),
        name="prep",
    )(qkv, qkv, qkv, qkv, qkv, qkv, qkv, qkv, qkv, conv_w, conv_w, conv_w, gcst)


def _scan_kernel(l1f, l2f, uf, glf, l1b, l2b, ub, glb, of_ref, ob_ref, s_ref, *, ncb, nh):
    @pl.when(pl.program_id(1) == 0)
    def _():
        s_ref[...] = jnp.zeros_like(s_ref)

    dirs = ((l1f, l2f, uf, glf, of_ref), (l1b, l2b, ub, glb, ob_ref))

    zero_rows = jnp.zeros((CHUNK, HEAD_DIM), BF16)

    def chunk_step(pair, parity):
        chains = [(d, h, pair[d], 2 * pair[d] + parity[d]) for d in range(N_DIR) for h in range(nh)]
        states = [s_ref[d, h] for d, h, _, _ in chains]
        r1s = [_dot(dirs[d][0][h, cc], s.astype(BF16))
               for (d, h, _, cc), s in zip(chains, states)]
        v_news = [dirs[d][2][h, cc].astype(F32) - r1[:CHUNK] for (d, h, _, cc), r1 in zip(chains, r1s)]
        r2s = []
        for (d, h, pc, _), v in zip(chains, v_news):
            vb = v.astype(BF16)
            rhs = jnp.concatenate([zero_rows, vb] if parity[d] else [vb, zero_rows], axis=0)
            r2s.append(_dot(dirs[d][1][h, pc], rhs))
        for (d, h, _, cc), s, r1, r2 in zip(chains, states, r1s, r2s):
            s_ref[d, h] = s * dirs[d][3][h, cc] + r2[CHUNK:]
            r0 = pl.multiple_of(cc * CHUNK, CHUNK)
            dirs[d][4][pl.ds(r0, CHUNK), h * HEAD_DIM:(h + 1) * HEAD_DIM] = (
                r1[CHUNK:] + r2[:CHUNK]).astype(dirs[d][4].dtype)

    def body(pi, carry):
        pair = (pi, ncb // 2 - 1 - pi)
        chunk_step(pair, (0, 1))
        chunk_step(pair, (1, 0))
        return carry

    lax.fori_loop(0, ncb // 2, body, 0)


def _scan(lhs1, lhs2, u, gl, *, seq):
    bsz, _, nh, nc = lhs1.shape[:4]
    rr = SCAN_ROWS
    ncb = rr // CHUNK
    nt = nc // ncb

    def specs(d):
        def mk(arr):
            tail = arr.shape[4:]
            zeros = (0,) * len(tail)
            if d == 0:
                imap = lambda b, t: (b, 0, 0, t) + zeros
            else:
                imap = lambda b, t: (b, 1, 0, nt - 1 - t) + zeros
            return pl.BlockSpec((None, None, nh, arr.shape[3] // nt) + tail, imap)
        return [mk(a) for a in (lhs1, lhs2, u, gl)]

    o_shape = jax.ShapeDtypeStruct((bsz * seq, nh * HEAD_DIM), BF16)
    return pl.pallas_call(
        functools.partial(_scan_kernel, ncb=ncb, nh=nh),
        grid=(bsz, nt),
        in_specs=specs(0) + specs(1),
        out_specs=[pl.BlockSpec((rr, nh * HEAD_DIM), lambda b, t: (b * nt + t, 0)),
                   pl.BlockSpec((rr, nh * HEAD_DIM), lambda b, t: (b * nt + nt - 1 - t, 0))],
        out_shape=[o_shape, o_shape],
        scratch_shapes=[pltpu.VMEM((N_DIR, nh, HEAD_DIM, HEAD_DIM), F32)],
        compiler_params=pltpu.CompilerParams(dimension_semantics=("arbitrary", "arbitrary"),
                                             vmem_limit_bytes=VMEM_LIMIT_BYTES),
        name="scan",
    )(lhs1, lhs2, u, gl, lhs1, lhs2, u, gl)


def _out_pipeline_kernel(x_hbm, of_hbm, ob_hbm, zs_hbm, p_hbm, gz_hbm, sga_hbm, sgb_hbm, gate_hbm, edge_hbm,
                         cwa_ref, gnw_ref, fnw_ref, wpa, wpb, wo, y_hbm, ext_ref, *, n_tiles, in_specs,
                         out_spec, **kw):
    def body(x_ref, of_ref, ob_ref, zs_ref, pm, pp, pn, gz_ref, sga_ref, sgb_ref, gate_ref, edge_ref, y_ref):
        _out_tile(x_ref, of_ref, ob_ref, zs_ref, pm, pp, pn, gz_ref, sga_ref, sgb_ref, gate_ref, edge_ref,
                  cwa_ref, gnw_ref, fnw_ref, wpa, wpb, wo, y_ref, ext_ref, **kw)

    pltpu.emit_pipeline(body, grid=(n_tiles,), in_specs=in_specs, out_specs=[out_spec])(
        x_hbm, of_hbm, ob_hbm, zs_hbm, p_hbm, p_hbm, p_hbm, gz_hbm, sga_hbm, sgb_hbm, gate_hbm, edge_hbm,
        y_hbm)


def _out_tile(x_ref, of_ref, ob_ref, zs_ref, pm, pp, pn, gz_ref, sga_ref, sgb_ref, gate_ref, edge_ref,
              cwa_ref, gnw_ref, fnw_ref, wpa, wpb, wo, y_ref, ext_ref, *, nh, final):
    has_prev = edge_ref[0, 0:1, :] > 0.5
    has_next = edge_ref[0, 1:2, :] > 0.5
    sub = x_ref.shape[0] // OUT_SUBTILES
    rows = [slice(i * sub, (i + 1) * sub) for i in range(OUT_SUBTILES)]

    pv = jnp.where(has_prev, pp[...].astype(F32), 0.0)
    nx = jnp.where(has_next, pn[...].astype(F32), 0.0)
    conv = _conv_rows(ext_ref, pm[...].astype(F32), pv, nx, cwa_ref, CONV_A_WIDTH)

    def yb_input(rs):
        o = of_ref[rs, :].astype(F32) + ob_ref[rs, :].astype(F32)
        heads = []
        for h in range(nh):
            oh = o[:, h * HEAD_DIM:(h + 1) * HEAD_DIM]
            heads.append(oh * lax.rsqrt(jnp.mean(oh * oh, axis=-1, keepdims=True) + NORM_EPS) * gnw_ref[...])
        return (jnp.concatenate(heads, axis=1) * zs_ref[rs, :].astype(F32)).astype(BF16)

    yb_ins = [yb_input(rs) for rs in rows]
    ya_ins = [(gz_ref[rs, :].astype(F32) * conv[rs]).astype(BF16) for rs in rows]
    ybs = [_dot(v, wpb[...]) for v in yb_ins]
    yas = [_dot(v, wpa[...]) for v in ya_ins]
    mergeds = [(sga_ref[rs, :].astype(F32) * ya + sgb_ref[rs, :].astype(F32) * yb).astype(BF16)
               for rs, ya, yb in zip(rows, yas, ybs)]
    deltas = [_dot(m, wo[...]) for m in mergeds]
    for rs, delta in zip(rows, deltas):
        xn = x_ref[rs, :] + gate_ref[0] * delta
        if final:
            xn = xn * lax.rsqrt(jnp.mean(xn * xn, axis=-1, keepdims=True) + NORM_EPS) * fnw_ref[...]
        y_ref[rs, :] = xn


def _out(x2, o_f, o_b, zs, p, gz, sga, sgb, gate, conv_a_w, gdn_norm_w, final_norm_w, wpa, wpb, wo,
         *, seq, nh, final):
    rows, d = x2.shape
    tm = OUT_ROWS
    tiles_per_seq = seq // tm
    row_blk = lambda w: pl.BlockSpec((tm, w), lambda i: (i, 0))
    row_in = lambda w: pl.BlockSpec((tm, w), lambda i: (i, 0), pipeline_mode=pl.Buffered(3))
    halo_prev, halo_next = _halo_specs(tm, p.shape[1], lambda i: 0, lambda i: i, rows)
    inner_specs = [row_in(d), row_in(o_f.shape[1]), row_in(o_b.shape[1]), row_in(zs.shape[1]),
                   row_in(p.shape[1]), halo_prev, halo_next, row_in(gz.shape[1]),
                   row_in(sga.shape[1]), row_in(sgb.shape[1]),
                   pl.BlockSpec((1, 1, d), lambda i: (i // tiles_per_seq, 0, 0)),
                   pl.BlockSpec((1, 2, p.shape[1]), lambda i: (i, 0, 0))]
    pos = jnp.arange(rows // tm) % tiles_per_seq
    edge = jnp.stack([pos > 0, pos < tiles_per_seq - 1], axis=1).astype(F32)
    edge = jnp.broadcast_to(edge[:, :, None], (rows // tm, 2, p.shape[1]))
    streamed = pl.BlockSpec(memory_space=pl.ANY)
    whole = pl.BlockSpec(memory_space=pltpu.VMEM)
    return pl.pallas_call(
        functools.partial(_out_pipeline_kernel, n_tiles=rows // tm, in_specs=inner_specs,
                          out_spec=row_blk(d), nh=nh, final=final),
        in_specs=[streamed] * 10 + [whole] * 6,
        out_specs=streamed,
        out_shape=jax.ShapeDtypeStruct((rows, d), F32),
        scratch_shapes=[pltpu.VMEM((tm + 2 * F32_SUBLANES, p.shape[1]), F32)],
        compiler_params=pltpu.CompilerParams(vmem_limit_bytes=VMEM_LIMIT_BYTES),
        name="out",
    )(x2, o_f, o_b, zs, p, gz, sga, sgb, gate, edge, conv_a_w, gdn_norm_w, final_norm_w, wpa, wpb, wo)


def kernel(x, c, w_ada, b_ada, norm_w, w_in, conv_a_w, conv_qkv_w, a_log, dt_bias, gdn_norm_w,
           w_pa, w_pb, w_o, final_norm_w):
    bsz, seq, d = x.shape
    depth = w_ada.shape[0]
    nh = a_log.shape[2]
    d_conv = conv_a_w.shape[2]
    key_dim = nh * HEAD_DIM
    assert conv_qkv_w.shape[2] == 3 * key_dim and gdn_norm_w.shape[1] == HEAD_DIM
    assert seq % PREP_ROWS == 0 and seq % SCAN_ROWS == 0 and seq % OUT_ROWS == 0
    n_gate = N_DIR * nh
    widths = dict(bg=d_conv, cg=d_conv, ax=d_conv, az=d_conv, qkv=3 * key_dim, z=key_dim,
                  ab=2 * n_gate, ga=d, gb=d)
    assert sum(widths.values()) == w_in.shape[2]
    offs, start = {}, 0
    for g, wd in widths.items():
        offs[g] = start
        start += wd
    assert all(o % BF16_SUBLANES == 0 for o in offs.values())

    x2 = x.reshape(bsz * seq, d)
    c_pad = jnp.zeros((8, d), F32).at[:bsz].set(c)
    for l in range(depth):
        mod = _ada(c_pad, w_ada[l], b_ada[l][None, :])[:bsz]
        shift, scale, gate = (mod[:, i * d:(i + 1) * d][:, None, :] for i in range(3))

        wt = w_in[l].T.astype(BF16)
        zeros = jnp.zeros((1, n_gate), F32)
        alog = jnp.concatenate([a_log[l].reshape(1, n_gate), zeros], axis=1)
        dtb = jnp.concatenate([dt_bias[l].reshape(1, n_gate), zeros], axis=1)
        p, gz, qkv, zs, sga, sgb, gcst = _inproj(
            x2, scale, shift, norm_w[l][None, :], wt, offs, widths, alog, dtb, seq=seq, nh=nh)

        lhs1, lhs2, u, gl = _prep(qkv, conv_qkv_w[l], gcst, bsz=bsz, seq=seq, nh=nh)
        o_f, o_b = _scan(lhs1, lhs2, u, gl, seq=seq)

        x2 = _out(x2, o_f, o_b, zs, p, gz, sga, sgb, gate, conv_a_w[l], gdn_norm_w[l][None, :],
                  final_norm_w[None, :], w_pa[l].astype(BF16), w_pb[l].astype(BF16), w_o[l].astype(BF16),
                  seq=seq, nh=nh, final=(l == depth - 1))
    return x2.reshape(bsz, seq, d)
```
